```python
import jax, jax.numpy as jnp
from jax import lax
import numpy as np

D_MODEL = 2048
BATCH = 4
SEQ = 4096
DEPTH = 2

GRID_W = 64
CTX_LEN = 256
N_MIXERS = 2
N_HEADS = 32
N_KV_HEADS = 8
HEAD_DIM = D_MODEL // N_HEADS
GROUP = N_HEADS // N_KV_HEADS
WINDOW = 128
BLOCK = 128
ROPE_BASE = 10000.0
CONV_WIDTH = 31
N_EXPERTS = 16
EC_FACTOR = 2
D_EXPERT = D_MODEL
N_ATTN_LAYERS = (DEPTH + N_MIXERS - 1) // N_MIXERS
N_CONV_LAYERS = DEPTH // N_MIXERS
EPS = 1e-6
NEG_INF = -1e30

kernel_name = "hybrid_swa_sink_conformer_ec_moe_dit"


def rmsnorm(x, g):
    xf = x.astype(jnp.float32)
    y = xf * lax.rsqrt(jnp.mean(xf * xf, axis=-1, keepdims=True) + EPS) * g.astype(jnp.float32)
    return y.astype(x.dtype)


def layernorm(x, g, b):
    xf = x.astype(jnp.float32)
    mu = jnp.mean(xf, axis=-1, keepdims=True)
    var = jnp.mean(jnp.square(xf - mu), axis=-1, keepdims=True)
    y = (xf - mu) * lax.rsqrt(var + EPS) * g.astype(jnp.float32) + b.astype(jnp.float32)
    return y.astype(x.dtype)


def modulate(h, shift, scale):
    return h * (1 + scale) + shift


def axial_rope_tables(n_tokens):
    rows = n_tokens // GRID_W
    r, col = jnp.meshgrid(jnp.arange(rows, dtype=jnp.float32),
                          jnp.arange(GRID_W, dtype=jnp.float32), indexing="ij")
    r, col = r.reshape(-1), col.reshape(-1)
    n_pairs = HEAD_DIM // 4
    inv = ROPE_BASE ** (-jnp.arange(n_pairs, dtype=jnp.float32) / n_pairs)
    ang_r = r[:, None] * inv[None, :]
    ang_c = col[:, None] * inv[None, :]
    ang = jnp.concatenate([ang_r, ang_r, ang_c, ang_c], axis=-1)
    return jnp.cos(ang), jnp.sin(ang)


def apply_rope(x, cos, sin):
    xf = x.astype(jnp.float32)
    x1, x2, x3, x4 = jnp.split(xf, 4, axis=-1)
    rot = jnp.concatenate([-x2, x1, -x4, x3], axis=-1)
    return (xf * cos[None, :, None, :] + rot * sin[None, :, None, :]).astype(x.dtype)


def banded_window_attention(q, k, v, kc, vc, sink_kg):
    B, S = q.shape[0], q.shape[1]
    L = kc.shape[1]
    nb = S // BLOCK
    scale = HEAD_DIM ** -0.5
    pad = ((0, 0), (BLOCK, BLOCK), (0, 0), (0, 0))
    kp = jnp.pad(k, pad)
    vp = jnp.pad(v, pad)
    qb = q.reshape(B, nb, BLOCK, N_KV_HEADS, GROUP, HEAD_DIM).transpose(1, 0, 2, 3, 4, 5)
    qi = jnp.arange(BLOCK)
    kj = jnp.arange(3 * BLOCK)
    n_loc = 3 * BLOCK

    def one_block(args):
        qblk, b = args
        start = b * BLOCK
        ks = lax.dynamic_slice_in_dim(kp, start, n_loc, axis=1)
        vs = lax.dynamic_slice_in_dim(vp, start, n_loc, axis=1)
        q_pos = start + qi
        k_pos = start - BLOCK + kj
        valid = ((jnp.abs(k_pos[None, :] - q_pos[:, None]) <= WINDOW)
                 & (k_pos >= 0)[None, :] & (k_pos < S)[None, :])
        s_loc = jnp.einsum("bqkgd,bskd->bkgqs", qblk, ks).astype(jnp.float32) * scale
        s_loc = jnp.where(valid[None, None, None], s_loc, NEG_INF)
        s_ctx = jnp.einsum("bqkgd,bskd->bkgqs", qblk, kc).astype(jnp.float32) * scale
        s_sink = jnp.broadcast_to(sink_kg[None, :, :, None, None], (B, N_KV_HEADS, GROUP, BLOCK, 1))
        p = jax.nn.softmax(jnp.concatenate([s_loc, s_ctx, s_sink], axis=-1), axis=-1).astype(v.dtype)
        o = (jnp.einsum("bkgqs,bskd->bqkgd", p[..., :n_loc], vs)
             + jnp.einsum("bkgqs,bskd->bqkgd", p[..., n_loc:n_loc + L], vc))
        return o

    o = lax.map(one_block, (qb, jnp.arange(nb)))
    return o.transpose(1, 0, 2, 3, 4, 5).reshape(B, S, N_HEADS * HEAD_DIM)


def context_attention(qc, kc, vc, sink_kg):
    B, L = qc.shape[0], qc.shape[1]
    qg = qc.reshape(B, L, N_KV_HEADS, GROUP, HEAD_DIM)
    s = jnp.einsum("bqkgd,bskd->bkgqs", qg, kc).astype(jnp.float32) * HEAD_DIM ** -0.5
    s_sink = jnp.broadcast_to(sink_kg[None, :, :, None, None], (B, N_KV_HEADS, GROUP, L, 1))
    p = jax.nn.softmax(jnp.concatenate([s, s_sink], axis=-1), axis=-1).astype(vc.dtype)
    o = jnp.einsum("bkgqs,bskd->bqkgd", p[..., :L], vc)
    return o.reshape(B, L, N_HEADS * HEAD_DIM)


def attention_mixer(h_lat, h_ctx, w_q, w_kv, w_o, q_gain, k_gain, sink, update_ctx):
    B, S, _ = h_lat.shape
    L = h_ctx.shape[1]
    q = (h_lat @ w_q).reshape(B, S, N_HEADS, HEAD_DIM)
    kv = (h_lat @ w_kv).reshape(B, S, 2, N_KV_HEADS, HEAD_DIM)
    cos, sin = axial_rope_tables(S)
    q = apply_rope(rmsnorm(q, q_gain), cos, sin)
    k = apply_rope(rmsnorm(kv[:, :, 0], k_gain), cos, sin)
    v = kv[:, :, 1]
    kvc = (h_ctx @ w_kv).reshape(B, L, 2, N_KV_HEADS, HEAD_DIM)
    kc = rmsnorm(kvc[:, :, 0], k_gain)
    vc = kvc[:, :, 1]
    sink_kg = sink.reshape(N_KV_HEADS, GROUP).astype(jnp.float32)
    out_lat = banded_window_attention(q, k, v, kc, vc, sink_kg) @ w_o
    if update_ctx:
        qc = rmsnorm((h_ctx @ w_q).reshape(B, L, N_HEADS, HEAD_DIM), q_gain)
        out_ctx = context_attention(qc, kc, vc, sink_kg) @ w_o
        return out_lat, out_ctx
    return out_lat, None


def conformer_conv(h, w_pw1, b_pw1, w_dw, b_dw, ln_g, ln_b, w_pw2, b_pw2):
    u = h @ w_pw1 + b_pw1
    a, g = jnp.split(u, 2, axis=-1)
    u = a * jax.nn.sigmoid(g)
    half = (CONV_WIDTH - 1) // 2
    u = lax.conv_general_dilated(u, w_dw[:, None, :], window_strides=(1,),
                                 padding=[(half, half)],
                                 dimension_numbers=("NWC", "WIO", "NWC"),
                                 feature_group_count=D_MODEL) + b_dw
    u = jax.nn.silu(layernorm(u, ln_g, ln_b))
    return u @ w_pw2 + b_pw2


def expert_choice_moe(h, w_router, w_gate, w_up, w_down):
    B, T, D = h.shape
    cap = EC_FACTOR * T // N_EXPERTS
    aff = jax.nn.softmax((h @ w_router).astype(jnp.float32), axis=-1)
    gate, idx = lax.top_k(jnp.swapaxes(aff, 1, 2), cap)
    idx_flat = idx.reshape(B, N_EXPERTS * cap)
    xs = jax.vmap(lambda hb, ib: hb[ib])(h, idx_flat).reshape(B, N_EXPERTS, cap, D)
    hid = (jax.nn.silu(jnp.einsum("becd,edf->becf", xs, w_gate))
           * jnp.einsum("becd,edf->becf", xs, w_up))
    y = jnp.einsum("becf,efd->becd", hid, w_down) * gate[..., None].astype(h.dtype)
    return jax.vmap(lambda yb, ib: jax.ops.segment_sum(yb, ib, num_segments=T))(
        y.reshape(B, N_EXPERTS * cap, D), idx_flat)


def setup_inputs(seed: int = 0) -> dict:
    key = jax.random.key(seed)
    ks = jax.random.split(key, 26)
    f32 = jnp.float32
    D = D_MODEL
    HD = N_HEADS * HEAD_DIM
    KVD = N_KV_HEADS * HEAD_DIM

    def nrm(k, shape, scale):
        return jax.random.normal(k, shape, f32) * scale

    return {
        "x": nrm(ks[0], (BATCH, SEQ, D), 1.0),
        "c": nrm(ks[1], (BATCH, D), 1.0),
        "ctx": nrm(ks[2], (BATCH, CTX_LEN, D), 1.0),
        "c_ctx": nrm(ks[3], (D,), 1.0),
        "ada_w": nrm(ks[4], (DEPTH, D, 6 * D), 0.5 * D ** -0.5),
        "ada_b": nrm(ks[5], (DEPTH, 6 * D), 0.02),
        "norm1_g": 1.0 + nrm(ks[6], (DEPTH, D), 0.02),
        "norm2_g": 1.0 + nrm(ks[7], (DEPTH, D), 0.02),
        "attn_w_q": nrm(ks[8], (N_ATTN_LAYERS, D, HD), D ** -0.5),
        "attn_w_kv": nrm(ks[9], (N_ATTN_LAYERS, D, 2 * KVD), D ** -0.5),
        "attn_w_o": nrm(ks[10], (N_ATTN_LAYERS, HD, D), HD ** -0.5),
        "attn_q_gain": 1.0 + nrm(ks[11], (N_ATTN_LAYERS, HEAD_DIM), 0.02),
        "attn_k_gain": 1.0 + nrm(ks[12], (N_ATTN_LAYERS, HEAD_DIM), 0.02),
        "attn_sink": nrm(ks[13], (N_ATTN_LAYERS, N_HEADS), 0.5),
        "conv_w_pw1": nrm(ks[14], (N_CONV_LAYERS, D, 2 * D), D ** -0.5),
        "conv_b_pw1": nrm(ks[15], (N_CONV_LAYERS, 2 * D), 0.02),
        "conv_w_dw": nrm(ks[16], (N_CONV_LAYERS, CONV_WIDTH, D), CONV_WIDTH ** -0.5),
        "conv_b_dw": nrm(ks[17], (N_CONV_LAYERS, D), 0.02),
        "conv_ln_g": 1.0 + nrm(ks[18], (N_CONV_LAYERS, D), 0.02),
        "conv_ln_b": nrm(ks[19], (N_CONV_LAYERS, D), 0.02),
        "conv_w_pw2": nrm(ks[20], (N_CONV_LAYERS, D, D), D ** -0.5),
        "conv_b_pw2": nrm(ks[21], (N_CONV_LAYERS, D), 0.02),
        "moe_router": nrm(ks[22], (DEPTH, D, N_EXPERTS), D ** -0.5),
        "moe_w_gate": nrm(ks[23], (DEPTH, N_EXPERTS, D, D_EXPERT), D ** -0.5),
        "moe_w_up": nrm(ks[24], (DEPTH, N_EXPERTS, D, D_EXPERT), D ** -0.5),
        "moe_w_down": nrm(ks[25], (DEPTH, N_EXPERTS, D_EXPERT, D), D_EXPERT ** -0.5),
    }


def reference(x, c, ctx, c_ctx, ada_w, ada_b, norm1_g, norm2_g,
              attn_w_q, attn_w_kv, attn_w_o, attn_q_gain, attn_k_gain, attn_sink,
              conv_w_pw1, conv_b_pw1, conv_w_dw, conv_b_dw, conv_ln_g, conv_ln_b,
              conv_w_pw2, conv_b_pw2, moe_router, moe_w_gate, moe_w_up, moe_w_down):
    ctx_readers = [i for i in range(DEPTH) if i % N_MIXERS == 0]
    last_reader = ctx_readers[-1]
    silu_c = jax.nn.silu(c)
    silu_cc = jax.nn.silu(c_ctx)
    h, hc = x, ctx

    def ctx_mod(i):
        m = (silu_cc @ ada_w[i] + ada_b[i])[None, None, :]
        return jnp.split(m, 6, axis=-1)

    for i in range(DEPTH):
        update_ctx = i < last_reader
        mod = (silu_c @ ada_w[i] + ada_b[i])[:, None, :]
        sh1, sc1, g1, sh2, sc2, g2 = jnp.split(mod, 6, axis=-1)
        hn = modulate(rmsnorm(h, norm1_g[i]), sh1, sc1)
        if i % N_MIXERS == 0:
            a = i // N_MIXERS
            sh1c, sc1c, g1c, sh2c, sc2c, g2c = ctx_mod(i)
            hcn = modulate(rmsnorm(hc, norm1_g[i]), sh1c, sc1c)
            out, out_c = attention_mixer(hn, hcn, attn_w_q[a], attn_w_kv[a], attn_w_o[a],
                                         attn_q_gain[a], attn_k_gain[a], attn_sink[a], update_ctx)
        else:
            ci = i // N_MIXERS
            conv_args = (conv_w_pw1[ci], conv_b_pw1[ci], conv_w_dw[ci], conv_b_dw[ci],
                         conv_ln_g[ci], conv_ln_b[ci], conv_w_pw2[ci], conv_b_pw2[ci])
            out = conformer_conv(hn, *conv_args)
            if update_ctx:
                sh1c, sc1c, g1c, sh2c, sc2c, g2c = ctx_mod(i)
                out_c = conformer_conv(modulate(rmsnorm(hc, norm1_g[i]), sh1c, sc1c), *conv_args)
        h = h + g1 * out
        h = h + g2 * expert_choice_moe(modulate(rmsnorm(h, norm2_g[i]), sh2, sc2),
                                       moe_router[i], moe_w_gate[i], moe_w_up[i], moe_w_down[i])
        if update_ctx:
            hc = hc + g1c * out_c
            hc = hc + g2c * expert_choice_moe(modulate(rmsnorm(hc, norm2_g[i]), sh2c, sc2c),
                                              moe_router[i], moe_w_gate[i], moe_w_up[i], moe_w_down[i])
    return h
```

```python
import functools

import jax
import jax.numpy as jnp
from jax import lax
from jax.experimental import pallas as pl
from jax.experimental.pallas import tpu as pltpu

F32 = jnp.float32
BF = jnp.bfloat16
I32 = jnp.int32
U32 = jnp.uint32

HEAD_DIM = 64
GROUP = 4
GRID_W = 64
ROPE_BASE = 10000.0
ATT_BLOCK = 128
CONV_WIDTH = 31
CONV_HALO = 16
N_EXPERTS = 16
EC_FACTOR = 2
EPS = 1e-6
NEG_INF = -1e30
LANES = 128
V7X_VMEM_LIMIT = 56 * 1024 * 1024


def _cparams(*sem):
    return pltpu.CompilerParams(dimension_semantics=sem, vmem_limit_bytes=V7X_VMEM_LIMIT)


def _dot(a, b):
    return jnp.dot(a, b, preferred_element_type=F32)


def _dot_nt(a, b):
    return lax.dot_general(a, b, (((1,), (1,)), ((), ())), preferred_element_type=F32)


def _split_bf16(x):
    hi = x.astype(BF)
    lo = (x - hi.astype(F32)).astype(BF)
    return hi, lo


def _norm_mod(x, g, sh, sc):
    ms = jnp.mean(x * x, axis=-1, keepdims=True)
    return (x * lax.rsqrt(ms + EPS) * g) * (1.0 + sc) + sh


def _ada_kernel(c_ref, w_ref, b_ref, o_ref):
    c = c_ref[...]
    s = c * jax.nn.sigmoid(c)
    o_ref[...] = _dot(s.astype(BF), w_ref[...].astype(BF)) + b_ref[...]


def _ada(cin, ada_w, ada_b):
    depth, d, n = ada_w.shape
    rows = cin.shape[0]
    tn = min(n, 1024)
    return pl.pallas_call(
        _ada_kernel,
        grid=(depth, n // tn),
        in_specs=[pl.BlockSpec((rows, d), lambda l, j: (0, 0)),
                  pl.BlockSpec((None, d, tn), lambda l, j: (l, 0, j)),
                  pl.BlockSpec((None, 1, tn), lambda l, j: (l, 0, j))],
        out_specs=pl.BlockSpec((None, rows, tn), lambda l, j: (l, 0, j)),
        out_shape=jax.ShapeDtypeStruct((depth, rows, n), F32),
        compiler_params=_cparams("arbitrary", "arbitrary"),
    )(cin, ada_w, ada_b.reshape(depth, 1, n))


def _head_norm(y, gain):
    r = lax.broadcasted_iota(I32, (LANES, LANES), 0) // HEAD_DIM
    c = lax.broadcasted_iota(I32, (LANES, LANES), 1) // HEAD_DIM
    bd = jnp.where(r == c, 1.0 / HEAD_DIM, 0.0).astype(BF)
    hi, lo = _split_bf16(y * y)
    ms = _dot(hi, bd) + _dot(lo, bd)
    return y * lax.rsqrt(ms + EPS) * gain


def _rope(y, cos, sin_a, sin_b):
    return (y * cos + pltpu.roll(y, LANES - HEAD_DIM // 4, 1) * sin_a
            + pltpu.roll(y, HEAD_DIM // 4, 1) * sin_b)


def _qkv_kernel(*refs, nh, nkv, rope, with_q):
    if with_q:
        (x_ref, g_ref, sh_ref, sc_ref, wq_ref, wkv_ref, qg_ref, kg_ref,
         cos_ref, sa_ref, sb_ref, q_ref, k_ref, v_ref) = refs
    else:
        (x_ref, g_ref, sh_ref, sc_ref, wkv_ref, kg_ref, k_ref, v_ref) = refs
    hb = _norm_mod(x_ref[...], g_ref[...], sh_ref[...], sc_ref[...]).astype(BF)
    if rope:
        cos, sin_a, sin_b = cos_ref[...], sa_ref[...], sb_ref[...]
    kvd = nkv * HEAD_DIM
    kv = _dot(hb, wkv_ref[...])
    for c in range(kvd // LANES):
        y = _head_norm(kv[:, c * LANES:(c + 1) * LANES], kg_ref[...])
        if rope:
            y = _rope(y, cos, sin_a, sin_b)
        yb = y.astype(BF)
        k_ref[2 * c] = yb[:, :HEAD_DIM]
        k_ref[2 * c + 1] = yb[:, HEAD_DIM:]
    for h in range(nkv):
        v_ref[h] = kv[:, kvd + h * HEAD_DIM:kvd + (h + 1) * HEAD_DIM].astype(BF)
    if with_q:
        cw = 2 * LANES
        for c in range(nh * HEAD_DIM // cw):
            qc = _dot(hb, wq_ref[:, c * cw:(c + 1) * cw])
            for half in range(2):
                y = _head_norm(qc[:, half * LANES:(half + 1) * LANES], qg_ref[...])
                if rope:
                    y = _rope(y, cos, sin_a, sin_b)
                yb = y.astype(BF)
                q_ref[4 * c + 2 * half] = yb[:, :HEAD_DIM]
                q_ref[4 * c + 2 * half + 1] = yb[:, HEAD_DIM:]


def _qkv_proj(x2, nb, g, sh, sc, wq, wkv, qg, kg, tables):
    m, d = x2.shape
    s = m // nb
    nkv = wkv.shape[1] // (2 * HEAD_DIM)
    with_q = wq is not None
    tm = min(s, 256)
    nt = s // tm
    shared = sh.shape[0] == 1
    mod_map = (lambda b, i: (0, 0, 0)) if shared else (lambda b, i: (b, 0, 0))
    row_spec = pl.BlockSpec((tm, d), lambda b, i: (b * nt + i, 0))
    vec_d = pl.BlockSpec((1, d), lambda b, i: (0, 0))
    mod_spec = pl.BlockSpec((None, 1, d), mod_map)
    vec_l = pl.BlockSpec((1, LANES), lambda b, i: (0, 0))
    tab_spec = pl.BlockSpec((tm, LANES), lambda b, i: (i, 0))
    kv_spec = pl.BlockSpec((None, nkv, tm, HEAD_DIM), lambda b, i: (b, 0, i, 0))
    kv_shape = jax.ShapeDtypeStruct((nb, nkv, s, HEAD_DIM), BF)
    if with_q:
        nh = wq.shape[1] // HEAD_DIM
        ins = [x2, g, sh, sc, wq, wkv, qg, kg, *tables]
        in_specs = [row_spec, vec_d, mod_spec, mod_spec,
                    pl.BlockSpec(wq.shape, lambda b, i: (0, 0)),
                    pl.BlockSpec(wkv.shape, lambda b, i: (0, 0)),
                    vec_l, vec_l, tab_spec, tab_spec, tab_spec]
        out_specs = [pl.BlockSpec((None, nh, tm, HEAD_DIM), lambda b, i: (b, 0, i, 0)), kv_spec, kv_spec]
        out_shape = [jax.ShapeDtypeStruct((nb, nh, s, HEAD_DIM), BF), kv_shape, kv_shape]
    else:
        nh = 0
        ins = [x2, g, sh, sc, wkv, kg]
        in_specs = [row_spec, vec_d, mod_spec, mod_spec,
                    pl.BlockSpec(wkv.shape, lambda b, i: (0, 0)), vec_l]
        out_specs = [kv_spec, kv_spec]
        out_shape = [kv_shape, kv_shape]
    return pl.pallas_call(
        functools.partial(_qkv_kernel, nh=nh, nkv=nkv, rope=with_q, with_q=with_q),
        grid=(nb, nt), in_specs=in_specs, out_specs=out_specs, out_shape=out_shape,
        compiler_params=_cparams("arbitrary", "arbitrary"),
    )(*ins)


def _attn_kernel(sink_ref, q_ref, kp_ref, kc_ref, kn_ref, vp_ref, vc_ref, vn_ref,
                 kx_ref, vx_ref, o_ref, *, nkv, nblk):
    i = pl.program_id(1)
    rows = GROUP * ATT_BLOCK
    qi = lax.broadcasted_iota(I32, (rows, ATT_BLOCK), 0) % ATT_BLOCK
    kj = lax.broadcasted_iota(I32, (rows, ATT_BLOCK), 1)
    prev_ok = (kj >= qi) & (i > 0)
    next_ok = (kj <= qi) & (i < nblk - 1)
    for h in range(nkv):
        q = q_ref[GROUP * h:GROUP * (h + 1)].reshape(rows, HEAD_DIM)
        s_p = jnp.where(prev_ok, _dot_nt(q, kp_ref[h]), NEG_INF)
        s_c = _dot_nt(q, kc_ref[h])
        s_n = jnp.where(next_ok, _dot_nt(q, kn_ref[h]), NEG_INF)
        s_x = _dot_nt(q, kx_ref[h])
        sink = jnp.concatenate(
            [jnp.full((ATT_BLOCK, 1), sink_ref[GROUP * h + g], F32) for g in range(GROUP)], axis=0)
        m = jnp.maximum(jnp.maximum(jnp.max(s_p, axis=-1, keepdims=True), jnp.max(s_c, axis=-1, keepdims=True)),
                        jnp.maximum(jnp.max(s_n, axis=-1, keepdims=True), jnp.max(s_x, axis=-1, keepdims=True)))
        m = jnp.maximum(m, sink)
        p_p, p_c, p_n, p_x = jnp.exp(s_p - m), jnp.exp(s_c - m), jnp.exp(s_n - m), jnp.exp(s_x - m)
        den = (jnp.sum(p_p, axis=-1, keepdims=True) + jnp.sum(p_c, axis=-1, keepdims=True)
               + jnp.sum(p_n, axis=-1, keepdims=True) + jnp.sum(p_x, axis=-1, keepdims=True)
               + jnp.exp(sink - m))
        o = (_dot(p_p.astype(BF), vp_ref[h]) + _dot(p_c.astype(BF), vc_ref[h])
             + _dot(p_n.astype(BF), vn_ref[h]) + _dot(p_x.astype(BF), vx_ref[h]))
        o = (o / den).astype(BF)
        for g in range(GROUP):
            hh = GROUP * h + g
            o_ref[:, hh * HEAD_DIM:(hh + 1) * HEAD_DIM] = o[g * ATT_BLOCK:(g + 1) * ATT_BLOCK]


def _attention(q, k, v, kx, vx, sink):
    nb, nh, s, _ = q.shape
    nkv = k.shape[1]
    lc = kx.shape[2]
    nblk = s // ATT_BLOCK
    blk = lambda f: pl.BlockSpec((None, nkv, ATT_BLOCK, HEAD_DIM), f)
    prev = lambda b, i: (b, 0, jnp.maximum(i - 1, 0), 0)
    cur = lambda b, i: (b, 0, i, 0)
    nxt = lambda b, i: (b, 0, jnp.minimum(i + 1, nblk - 1), 0)
    ctx_spec = pl.BlockSpec((None, nkv, lc, HEAD_DIM), lambda b, i: (b, 0, 0, 0))
    return pl.pallas_call(
        functools.partial(_attn_kernel, nkv=nkv, nblk=nblk),
        grid=(nb, nblk),
        in_specs=[pl.BlockSpec(memory_space=pltpu.SMEM),
                  pl.BlockSpec((None, nh, ATT_BLOCK, HEAD_DIM), cur),
                  blk(prev), blk(cur), blk(nxt), blk(prev), blk(cur), blk(nxt), ctx_spec, ctx_spec],
        out_specs=pl.BlockSpec((ATT_BLOCK, nh * HEAD_DIM), lambda b, i: (b * nblk + i, 0)),
        out_shape=jax.ShapeDtypeStruct((nb * s, nh * HEAD_DIM), BF),
        compiler_params=_cparams("arbitrary", "arbitrary"),
    )(sink, q, k, k, k, v, v, v, kx, vx)


def _mm_res_kernel(a_ref, w_ref, res_ref, gate_ref, o_ref):
    o_ref[...] = res_ref[...] + gate_ref[...] * _dot(a_ref[...], w_ref[...])


def _mm_res(a, w, res, gate, nb):
    m, kd = a.shape
    n = w.shape[1]
    tm = min(m // nb, 256)
    per_b = m // nb // tm
    return pl.pallas_call(
        _mm_res_kernel,
        grid=(m // tm,),
        in_specs=[pl.BlockSpec((tm, kd), lambda i: (i, 0)),
                  pl.BlockSpec((kd, n), lambda i: (0, 0)),
                  pl.BlockSpec((tm, n), lambda i: (i, 0)),
                  pl.BlockSpec((None, 1, n), lambda i: (i // per_b, 0, 0))],
        out_specs=pl.BlockSpec((tm, n), lambda i: (i, 0)),
        out_shape=jax.ShapeDtypeStruct((m, n), F32),
        compiler_params=_cparams("arbitrary"),
    )(a, w, res, gate)


def _pw1_kernel(x_ref, g_ref, sh_ref, sc_ref, wa_ref, wg_ref, ba_ref, bg_ref, o_ref):
    hb = _norm_mod(x_ref[...], g_ref[...], sh_ref[...], sc_ref[...]).astype(BF)
    a = _dot(hb, wa_ref[...]) + ba_ref[...]
    gt = _dot(hb, wg_ref[...]) + bg_ref[...]
    o_ref[...] = a * jax.nn.sigmoid(gt)


def _pw1_glu(x2, nb, g, sh, sc, w, bias):
    m, d = x2.shape
    tm = min(m // nb, 256)
    per_b = m // nb // tm
    tn = min(d, 1024)
    nj = d // tn
    mod_spec = pl.BlockSpec((None, 1, d), lambda j, i: (i // per_b, 0, 0))
    return pl.pallas_call(
        _pw1_kernel,
        grid=(nj, m // tm),
        in_specs=[pl.BlockSpec((tm, d), lambda j, i: (i, 0)),
                  pl.BlockSpec((1, d), lambda j, i: (0, 0)), mod_spec, mod_spec,
                  pl.BlockSpec((d, tn), lambda j, i: (0, j)),
                  pl.BlockSpec((d, tn), lambda j, i: (0, nj + j)),
                  pl.BlockSpec((1, tn), lambda j, i: (0, j)),
                  pl.BlockSpec((1, tn), lambda j, i: (0, nj + j))],
        out_specs=pl.BlockSpec((tm, tn), lambda j, i: (i, j)),
        out_shape=jax.ShapeDtypeStruct((m, d), F32),
        compiler_params=_cparams("arbitrary", "arbitrary"),
    )(x2, g, sh, sc, w, w, bias, bias)


def _conv_kernel(up_ref, uc_ref, un_ref, wdw_ref, bdw_ref, lg_ref, lb_ref, w2_ref, b2_ref,
                 res_ref, gate_ref, o_ref, ext_ref, cv_ref, *, nt):
    i = pl.program_id(1)
    tm, d = uc_ref.shape
    zero = jnp.zeros((CONV_HALO, d), F32)
    ext_ref[0:CONV_HALO, :] = jnp.where(i > 0, up_ref[...], zero)
    ext_ref[CONV_HALO:CONV_HALO + tm, :] = uc_ref[...]
    ext_ref[CONV_HALO + tm:, :] = jnp.where(i < nt - 1, un_ref[...], zero)
    off = CONV_HALO - CONV_WIDTH // 2
    for c in range(d // LANES):
        cols = slice(c * LANES, (c + 1) * LANES)
        acc = jnp.broadcast_to(bdw_ref[:, cols], (tm, LANES))
        for k in range(CONV_WIDTH):
            acc = acc + wdw_ref[k:k + 1, cols] * ext_ref[off + k:off + k + tm, cols]
        cv_ref[:, cols] = acc
    u = cv_ref[...]
    mu = jnp.mean(u, axis=-1, keepdims=True)
    var = jnp.mean(jnp.square(u - mu), axis=-1, keepdims=True)
    y = (u - mu) * lax.rsqrt(var + EPS) * lg_ref[...] + lb_ref[...]
    y = y * jax.nn.sigmoid(y)
    out = _dot(y.astype(BF), w2_ref[...]) + b2_ref[...]
    o_ref[...] = res_ref[...] + gate_ref[...] * out


def _conv_pw2(u, nb, wdw, bdw, lg, lb, w2, b2, res, gate):
    m, d = u.shape
    s = m // nb
    tm = min(s, 256)
    nt = s // tm
    hb = tm // CONV_HALO
    nhb = s // CONV_HALO
    vec = pl.BlockSpec((1, d), lambda b, i: (0, 0))
    tile = pl.BlockSpec((tm, d), lambda b, i: (b * nt + i, 0))
    return pl.pallas_call(
        functools.partial(_conv_kernel, nt=nt),
        grid=(nb, nt),
        in_specs=[pl.BlockSpec((CONV_HALO, d), lambda b, i: (b * nhb + jnp.maximum(i * hb - 1, 0), 0)),
                  tile,
                  pl.BlockSpec((CONV_HALO, d), lambda b, i: (b * nhb + jnp.minimum((i + 1) * hb, nhb - 1), 0)),
                  pl.BlockSpec((CONV_WIDTH, d), lambda b, i: (0, 0)),
                  vec, vec, vec,
                  pl.BlockSpec((d, d), lambda b, i: (0, 0)),
                  vec, tile,
                  pl.BlockSpec((None, 1, d), lambda b, i: (b, 0, 0))],
        out_specs=tile,
        out_shape=jax.ShapeDtypeStruct((m, d), F32),
        scratch_shapes=[pltpu.VMEM((tm + 2 * CONV_HALO, d), F32), pltpu.VMEM((tm, d), F32)],
        compiler_params=_cparams("arbitrary", "arbitrary"),
    )(u, u, u, wdw, bdw, lg, lb, w2, b2, res, gate)


def _router_kernel(x_ref, g_ref, sh_ref, sc_ref, wr_ref, xp_ref, aff_ref):
    hn = _norm_mod(x_ref[...], g_ref[...], sh_ref[...], sc_ref[...])
    h_hi, h_lo = _split_bf16(hn)
    w_hi, w_lo = _split_bf16(wr_ref[...])
    logits = _dot_nt(w_hi, h_hi) + _dot_nt(w_hi, h_lo) + _dot_nt(w_lo, h_hi)
    mx = jnp.max(logits, axis=0, keepdims=True)
    ex = jnp.exp(logits - mx)
    aff_ref[...] = ex / jnp.sum(ex, axis=0, keepdims=True)
    bits = lax.bitcast_convert_type(h_hi.astype(F32), U32)
    d2 = bits.shape[1] // 2
    xp_ref[...] = bits[:, :d2] | (bits[:, d2:] >> 16)


def _router(x2, nb, g, sh, sc, wr_t):
    m, d = x2.shape
    s = m // nb
    ne = wr_t.shape[0]
    tm = min(s, 512)
    nt = s // tm
    mod_spec = pl.BlockSpec((None, 1, d), lambda b, i: (b, 0, 0))
    return pl.pallas_call(
        _router_kernel,
        grid=(nb, nt),
        in_specs=[pl.BlockSpec((tm, d), lambda b, i: (b * nt + i, 0)),
                  pl.BlockSpec((1, d), lambda b, i: (0, 0)), mod_spec, mod_spec,
                  pl.BlockSpec((ne, d), lambda b, i: (0, 0))],
        out_specs=[pl.BlockSpec((tm, d // 2), lambda b, i: (b * nt + i, 0)),
                   pl.BlockSpec((None, ne, tm), lambda b, i: (b, 0, i))],
        out_shape=[jax.ShapeDtypeStruct((m, d // 2), U32),
                   jax.ShapeDtypeStruct((nb, ne, s), F32)],
        compiler_params=_cparams("arbitrary", "arbitrary"),
    )(x2, g, sh, sc, wr_t)


def _prefix_incl(mask_bf):
    ne, s = mask_bf.shape
    r = lax.broadcasted_iota(I32, (LANES, LANES), 0)
    c = lax.broadcasted_iota(I32, (LANES, LANES), 1)
    tri = jnp.where(r <= c, 1.0, 0.0).astype(BF)
    carry = jnp.zeros((ne, 1), F32)
    out = []
    for ch in range(s // LANES):
        inc = _dot(mask_bf[:, ch * LANES:(ch + 1) * LANES], tri) + carry
        out.append(inc)
        carry = inc[:, LANES - 1:LANES]
    return jnp.concatenate(out, axis=1)


def _topk_kernel(aff_ref, pos_ref, gate_ref, cnt_ref, idx_ref, pos_scr, *, cap):
    aff = aff_ref[...]
    ne, s = aff.shape
    v = lax.bitcast_convert_type(aff, I32)

    def bit_step(it, prefix):
        cand = prefix | jnp.left_shift(jnp.int32(1), 30 - it)
        cnt = jnp.sum(jnp.where(v >= cand, 1.0, 0.0), axis=1, keepdims=True)
        return jnp.where(cnt >= cap, cand, prefix)

    thr = lax.fori_loop(0, 31, bit_step, jnp.zeros((ne, 1), I32))
    gt = v > thr
    eq = v == thr
    need = cap - jnp.sum(jnp.where(gt, 1.0, 0.0), axis=1, keepdims=True)
    eq_bf = jnp.where(eq, 1.0, 0.0).astype(BF)
    eq_rank = _prefix_incl(eq_bf) - eq_bf.astype(F32)
    sel = gt | (eq & (eq_rank < need))
    sel_bf = jnp.where(sel, 1.0, 0.0).astype(BF)
    incl = _prefix_incl(sel_bf)
    pos = jnp.where(sel, incl - 1.0, -1.0).astype(I32)
    pos_ref[...] = pos
    gate_ref[...] = jnp.where(sel, aff, 0.0)
    t_i = lax.broadcasted_iota(I32, (s, LANES), 0)
    j_i = lax.broadcasted_iota(I32, (s, LANES), 1)
    before = jnp.where(t_i < j_i * LANES, 1.0, 0.0).astype(BF)
    cnt_ref[...] = _dot(sel_bf, before).astype(I32)
    pos_scr[...] = pos
    t_row = lax.broadcasted_iota(I32, (8, s), 1)
    r_row = lax.broadcasted_iota(I32, (8, s), 0)
    tval = jnp.where(r_row == 0, t_row // 64, jnp.where(r_row == 1, t_row % 64, 0)).astype(F32).astype(BF)
    slot = lax.broadcasted_iota(I32, (cap, 1), 0)

    def expert_step(e, carry):
        onehot = jnp.where(pos_scr[pl.ds(e, 1), :] == slot, 1.0, 0.0).astype(BF)
        res = _dot_nt(tval, onehot)
        idx_ref[pl.ds(e, 1), :] = (res[0:1] * 64.0 + res[1:2]).astype(I32)
        return carry

    lax.fori_loop(0, ne, expert_step, 0)


def _topk(aff_t, cap):
    nb, ne, s = aff_t.shape
    blk = pl.BlockSpec((None, ne, s), lambda b: (b, 0, 0))
    return pl.pallas_call(
        functools.partial(_topk_kernel, cap=cap),
        grid=(nb,),
        in_specs=[blk],
        out_specs=[blk, blk,
                   pl.BlockSpec((None, ne, LANES), lambda b: (b, 0, 0)),
                   pl.BlockSpec((None, ne, cap), lambda b: (b, 0, 0))],
        out_shape=[jax.ShapeDtypeStruct((nb, ne, s), I32),
                   jax.ShapeDtypeStruct((nb, ne, s), F32),
                   jax.ShapeDtypeStruct((nb, ne, LANES), I32),
                   jax.ShapeDtypeStruct((nb, ne, cap), I32)],
        scratch_shapes=[pltpu.VMEM((ne, s), I32)],
        compiler_params=_cparams("arbitrary"),
    )(aff_t)


def _row_copy(src_ref, src_row, dst_ref, dst_row, sem):
    return pltpu.make_async_copy(src_ref.at[pl.ds(src_row, 1), :], dst_ref.at[pl.ds(dst_row, 1), :], sem)


def _gather_kernel(idx_ref, src_ref, dst_ref, sem, *, n):
    e = pl.program_id(0)

    def issue(c, carry):
        _row_copy(src_ref, idx_ref[0, c], dst_ref, e * n + c, sem).start()
        return carry

    lax.fori_loop(0, n, issue, 0)

    def drain(c, carry):
        _row_copy(src_ref, 0, dst_ref, e * n + c, sem).wait()
        return carry

    lax.fori_loop(0, n, drain, 0)


def _gather(xp, rows):
    ne, _, n = rows.shape
    w = xp.shape[1]
    return pl.pallas_call(
        functools.partial(_gather_kernel, n=n),
        grid=(ne,),
        in_specs=[pl.BlockSpec((None, 1, n), lambda e: (e, 0, 0), memory_space=pltpu.SMEM),
                  pl.BlockSpec(memory_space=pl.ANY)],
        out_specs=pl.BlockSpec(memory_space=pl.ANY),
        out_shape=jax.ShapeDtypeStruct((ne * n, w), xp.dtype),
        scratch_shapes=[pltpu.SemaphoreType.DMA(())],
        compiler_params=_cparams("arbitrary"),
    )(rows, xp)


def _expert_up_kernel(x_ref, wg_ref, wu_ref, o_ref, xb_ref):
    @pl.when(pl.program_id(1) == 0)
    def _():
        w = x_ref[...]
        d2 = w.shape[1]
        xb_ref[:, :d2] = lax.bitcast_convert_type(w & jnp.uint32(0xFFFF0000), F32).astype(BF)
        xb_ref[:, d2:] = lax.bitcast_convert_type(w << 16, F32).astype(BF)

    x = xb_ref[...]
    g = _dot(x, wg_ref[...].astype(BF))
    u = _dot(x, wu_ref[...].astype(BF))
    o_ref[...] = (g * jax.nn.sigmoid(g) * u).astype(BF)


def _expert_up(xg, w_gate, w_up, layer):
    _, ne, d, f = w_gate.shape
    n = xg.shape[0] // ne
    tf = min(f, 256)
    w_spec = pl.BlockSpec((None, None, d, tf), lambda e, j: (layer, e, 0, j))
    return pl.pallas_call(
        _expert_up_kernel,
        grid=(ne, f // tf),
        in_specs=[pl.BlockSpec((None, n, d // 2), lambda e, j: (e, 0, 0)), w_spec, w_spec],
        out_specs=pl.BlockSpec((None, n, tf), lambda e, j: (e, 0, j)),
        out_shape=jax.ShapeDtypeStruct((ne, n, f), BF),
        scratch_shapes=[pltpu.VMEM((n, d), BF)],
        compiler_params=_cparams("arbitrary", "arbitrary"),
    )(xg.reshape(ne, n, d // 2), w_gate, w_up)


def _expert_down_kernel(h_ref, w_ref, o_ref):
    o_ref[...] = _dot(h_ref[...], w_ref[...].astype(BF))


def _expert_down(hid, w_down, layer):
    ne, n, f = hid.shape
    d = w_down.shape[3]
    tn = min(d, 256)
    return pl.pallas_call(
        _expert_down_kernel,
        grid=(ne, d // tn),
        in_specs=[pl.BlockSpec((None, n, f), lambda e, j: (e, 0, 0)),
                  pl.BlockSpec((None, None, f, tn), lambda e, j: (layer, e, 0, j))],
        out_specs=pl.BlockSpec((None, n, tn), lambda e, j: (e, 0, j)),
        out_shape=jax.ShapeDtypeStruct((ne, n, d), F32),
        compiler_params=_cparams("arbitrary", "arbitrary"),
    )(hid, w_down)


COMBINE_TOKENS = 128
COMBINE_CHUNK = 256


def _combine_kernel(cnt_ref, y_ref, pos_ref, gt_ref, res_ref, g2_ref, o_ref, buf_ref, acc_ref, sem,
                    *, ne, n, cap, nbound):
    b = pl.program_id(0)
    t = pl.program_id(1)

    @pl.when((b == 0) & (t == 0))
    def _():
        buf_ref[...] = jnp.zeros_like(buf_ref)

    shift = []
    total = jnp.int32(0)
    for e in range(ne):
        base = (b * ne + e) * nbound + t
        c0 = cnt_ref[base]
        c1 = cnt_ref[base + 1]
        row0 = e * n + b * cap
        slot0 = total - c0

        def issue(c, carry, row0=row0, slot0=slot0):
            _row_copy(y_ref, row0 + c, buf_ref, slot0 + c, sem).start()
            return carry

        lax.fori_loop(c0, c1, issue, 0)
        shift.append(slot0)
        total = total + (c1 - c0)

    def drain(c, carry):
        _row_copy(y_ref, 0, buf_ref, c, sem).wait()
        return carry

    lax.fori_loop(0, total, drain, 0)

    acc_ref[...] = jnp.zeros_like(acc_ref)
    pos = pos_ref[...]
    gts = gt_ref[...]

    def chunk(ci, carry):
        start = pl.multiple_of(ci * COMBINE_CHUNK, COMBINE_CHUNK)
        lane = lax.broadcasted_iota(I32, (COMBINE_TOKENS, COMBINE_CHUNK), 1) + start
        gmat = jnp.zeros((COMBINE_TOKENS, COMBINE_CHUNK), F32)
        for e in range(ne):
            pe = pos[:, e:e + 1]
            gmat = jnp.where((lane == pe + shift[e]) & (pe >= 0), gts[:, e:e + 1], gmat)
        live = lax.broadcasted_iota(I32, (COMBINE_CHUNK, 1), 0) + start < total
        rows = jnp.where(live, buf_ref[pl.ds(start, COMBINE_CHUNK), :], 0.0)
        g_hi, g_lo = _split_bf16(gmat)
        r_hi, r_lo = _split_bf16(rows)
        acc_ref[...] += _dot(g_hi, r_hi) + _dot(g_hi, r_lo) + _dot(g_lo, r_hi)
        return carry

    lax.fori_loop(0, (total + COMBINE_CHUNK - 1) // COMBINE_CHUNK, chunk, 0)
    o_ref[...] = res_ref[...] + g2_ref[...] * acc_ref[...]


def _combine(cnt, y, pos_t, gate_t, res, g2, nb, cap):
    m, d = res.shape
    s = m // nb
    ne = pos_t.shape[1]
    n = nb * cap
    nt = s // COMBINE_TOKENS
    nbound = nt + 1
    tok = lambda b, t, c: (b * nt + t, 0)
    grid_spec = pltpu.PrefetchScalarGridSpec(
        num_scalar_prefetch=1,
        grid=(nb, nt),
        in_specs=[pl.BlockSpec(memory_space=pl.ANY),
                  pl.BlockSpec((COMBINE_TOKENS, ne), tok),
                  pl.BlockSpec((COMBINE_TOKENS, ne), tok),
                  pl.BlockSpec((COMBINE_TOKENS, d), tok),
                  pl.BlockSpec((None, 1, d), lambda b, t, c: (b, 0, 0))],
        out_specs=pl.BlockSpec((COMBINE_TOKENS, d), tok),
        scratch_shapes=[pltpu.VMEM((ne * COMBINE_TOKENS, d), F32),
                        pltpu.VMEM((COMBINE_TOKENS, d), F32),
                        pltpu.SemaphoreType.DMA(())],
    )
    return pl.pallas_call(
        functools.partial(_combine_kernel, ne=ne, n=n, cap=cap, nbound=nbound),
        grid_spec=grid_spec,
        out_shape=jax.ShapeDtypeStruct((m, d), F32),
        compiler_params=_cparams("arbitrary", "arbitrary"),
    )(cnt[:, :, :nbound].reshape(-1), y, pos_t, gate_t, res, g2)


def _moe(h2, nb, g, sh, sc, gate2, w_router, w_gate, w_up, w_down, layer):
    m, d = h2.shape
    s = m // nb
    ne = w_router.shape[1]
    cap = EC_FACTOR * s // ne
    xp, aff_t = _router(h2, nb, g, sh, sc, w_router.T)
    pos, gate, cnt, idx = _topk(aff_t, cap)
    rows = idx + (jnp.arange(nb, dtype=I32) * s)[:, None, None]
    rows = jnp.swapaxes(rows, 0, 1).reshape(ne, 1, nb * cap)
    xg = _gather(xp, rows)
    hid = _expert_up(xg, w_gate, w_up, layer)
    y = _expert_down(hid, w_down, layer).reshape(ne * nb * cap, d)
    pos_t = jnp.swapaxes(pos, 1, 2).reshape(m, ne)
    gate_t = jnp.swapaxes(gate, 1, 2).reshape(m, ne)
    return _combine(cnt, y, pos_t, gate_t, h2, gate2, nb, cap)


def _rope_tables(s):
    rows = s // GRID_W
    r, col = jnp.meshgrid(jnp.arange(rows, dtype=F32), jnp.arange(GRID_W, dtype=F32), indexing="ij")
    r, col = r.reshape(-1), col.reshape(-1)
    n_pairs = HEAD_DIM // 4
    inv = ROPE_BASE ** (-jnp.arange(n_pairs, dtype=F32) / n_pairs)
    ang_r = r[:, None] * inv[None, :]
    ang_c = col[:, None] * inv[None, :]
    ang = jnp.concatenate([ang_r, ang_r, ang_c, ang_c], axis=-1)
    cos, sin = jnp.cos(ang), jnp.sin(ang)
    first = (jnp.arange(HEAD_DIM) % (HEAD_DIM // 2)) < HEAD_DIM // 4
    sin_a = jnp.where(first, -sin, 0.0)
    sin_b = jnp.where(first, 0.0, sin)
    two = lambda a: jnp.concatenate([a, a], axis=-1)
    return two(cos), two(sin_a), two(sin_b)


def kernel(x, c, ctx, c_ctx, ada_w, ada_b, norm1_g, norm2_g, attn_w_q, attn_w_kv, attn_w_o, attn_q_gain, attn_k_gain, attn_sink, conv_w_pw1, conv_b_pw1, conv_w_dw, conv_b_dw, conv_ln_g, conv_ln_b, conv_w_pw2, conv_b_pw2, moe_router, moe_w_gate, moe_w_up, moe_w_down):
    nb, s, d = x.shape
    lc = ctx.shape[1]
    depth = ada_w.shape[0]
    assert depth == 2, "layer 0 attention, layer 1 convolution; the context stream is never updated"
    m = nb * s

    cin = jnp.concatenate([c, c_ctx[None, :], jnp.zeros((8 - nb - 1, d), F32)], axis=0)
    mod = _ada(cin, ada_w, ada_b)

    def mods(layer, row0, nrow):
        return [mod[layer, row0:row0 + nrow, k * d:(k + 1) * d].reshape(nrow, 1, d) for k in range(6)]

    h = x.reshape(m, d)
    two = lambda a: jnp.concatenate([a, a], axis=-1).reshape(1, LANES)

    sh1, sc1, g1, sh2, sc2, g2 = mods(0, 0, nb)
    sh1c, sc1c = mods(0, nb, 1)[:2]
    n1 = norm1_g[0].reshape(1, d)
    wq = attn_w_q[0].astype(BF)
    wkv = attn_w_kv[0].astype(BF)
    wo = attn_w_o[0].astype(BF)
    q_gain = two(attn_q_gain[0]) * (HEAD_DIM ** -0.5)
    k_gain = two(attn_k_gain[0])
    q, k, v = _qkv_proj(h, nb, n1, sh1, sc1, wq, wkv, q_gain, k_gain, _rope_tables(s))
    kx, vx = _qkv_proj(ctx.reshape(nb * lc, d), nb, n1, sh1c, sc1c, None, wkv, None, k_gain, None)
    att = _attention(q, k, v, kx, vx, attn_sink[0])
    h = _mm_res(att, wo, h, g1, nb)
    h = _moe(h, nb, norm2_g[0].reshape(1, d), sh2, sc2, g2,
             moe_router[0], moe_w_gate, moe_w_up, moe_w_down, 0)

    sh1, sc1, g1, sh2, sc2, g2 = mods(1, 0, nb)
    u = _pw1_glu(h, nb, norm1_g[1].reshape(1, d), sh1, sc1,
                 conv_w_pw1[0].astype(BF), conv_b_pw1[0].reshape(1, 2 * d))
    h = _conv_pw2(u, nb, conv_w_dw[0], conv_b_dw[0].reshape(1, d), conv_ln_g[0].reshape(1, d),
                  conv_ln_b[0].reshape(1, d), conv_w_pw2[0].astype(BF), conv_b_pw2[0].reshape(1, d), h, g1)
    h = _moe(h, nb, norm2_g[1].reshape(1, d), sh2, sc2, g2,
             moe_router[1], moe_w_gate, moe_w_up, moe_w_down, 1)
    return h.reshape(nb, s, d)
```

```python
import functools

import jax
import jax.numpy as jnp
from jax import lax
from jax.experimental import pallas as pl
from jax.experimental.pallas import tpu as pltpu

F32 = jnp.float32
BF = jnp.bfloat16
I32 = jnp.int32
U32 = jnp.uint32

HEAD_DIM = 64
GROUP = 4
GRID_W = 64
ROPE_BASE = 10000.0
ATT_BLOCK = 128
CONV_WIDTH = 31
CONV_HALO = 16
CONV_ROWS = 64
SUBLANES = 8
N_EXPERTS = 16
EC_FACTOR = 2
EPS = 1e-6
NEG_INF = -1e30
LOG2E = 1.4426950408889634
LANES = 128
V7X_VMEM_LIMIT = 56 * 1024 * 1024


def _cparams(*sem):
    return pltpu.CompilerParams(dimension_semantics=sem, vmem_limit_bytes=V7X_VMEM_LIMIT)


def _dot(a, b):
    return jnp.dot(a, b, preferred_element_type=F32)


def _dot_nt(a, b):
    return lax.dot_general(a, b, (((1,), (1,)), ((), ())), preferred_element_type=F32)


def _split_bf16(x):
    hi = x.astype(BF)
    lo = (x - hi.astype(F32)).astype(BF)
    return hi, lo


def _norm_mod(x, g, sh, sc):
    ms = jnp.mean(x * x, axis=-1, keepdims=True)
    return (x * lax.rsqrt(ms + EPS) * g) * (1.0 + sc) + sh


def _ada_kernel(c_ref, w_ref, b_ref, o_ref):
    c = c_ref[...]
    s = c * jax.nn.sigmoid(c)
    o_ref[...] = _dot(s.astype(BF), w_ref[...].astype(BF)) + b_ref[...]


def _ada(cin, ada_w, ada_b):
    depth, d, n = ada_w.shape
    rows = cin.shape[0]
    tn = min(n, 1024)
    return pl.pallas_call(
        _ada_kernel,
        grid=(depth, n // tn),
        in_specs=[pl.BlockSpec((rows, d), lambda l, j: (0, 0)),
                  pl.BlockSpec((None, d, tn), lambda l, j: (l, 0, j)),
                  pl.BlockSpec((None, 1, tn), lambda l, j: (l, 0, j))],
        out_specs=pl.BlockSpec((None, rows, tn), lambda l, j: (l, 0, j)),
        out_shape=jax.ShapeDtypeStruct((depth, rows, n), F32),
        compiler_params=_cparams("arbitrary", "arbitrary"),
    )(cin, ada_w, ada_b.reshape(depth, 1, n))


def _head_norm(y, gain):
    r = lax.broadcasted_iota(I32, (LANES, LANES), 0) // HEAD_DIM
    c = lax.broadcasted_iota(I32, (LANES, LANES), 1) // HEAD_DIM
    bd = jnp.where(r == c, 1.0 / HEAD_DIM, 0.0).astype(BF)
    hi, lo = _split_bf16(y * y)
    ms = _dot(hi, bd) + _dot(lo, bd)
    return y * lax.rsqrt(ms + EPS) * gain


def _rope(y, cos, sin_a, sin_b):
    return (y * cos + pltpu.roll(y, LANES - HEAD_DIM // 4, 1) * sin_a
            + pltpu.roll(y, HEAD_DIM // 4, 1) * sin_b)


def _qkv_kernel(*refs, nh, nkv, rope, with_q):
    if with_q:
        (x_ref, g_ref, sh_ref, sc_ref, wq_ref, wkv_ref, qg_ref, kg_ref,
         cos_ref, sa_ref, sb_ref, q_ref, k_ref, v_ref) = refs
    else:
        (x_ref, g_ref, sh_ref, sc_ref, wkv_ref, kg_ref, k_ref, v_ref) = refs
    hb = _norm_mod(x_ref[...], g_ref[...], sh_ref[...], sc_ref[...]).astype(BF)
    if rope:
        cos, sin_a, sin_b = cos_ref[...], sa_ref[...], sb_ref[...]
    kvd = nkv * HEAD_DIM
    kv = _dot(hb, wkv_ref[...])
    for c in range(kvd // LANES):
        y = _head_norm(kv[:, c * LANES:(c + 1) * LANES], kg_ref[...])
        if rope:
            y = _rope(y, cos, sin_a, sin_b)
        yb = y.astype(BF)
        k_ref[2 * c] = yb[:, :HEAD_DIM]
        k_ref[2 * c + 1] = yb[:, HEAD_DIM:]
    for h in range(nkv):
        v_ref[h] = kv[:, kvd + h * HEAD_DIM:kvd + (h + 1) * HEAD_DIM].astype(BF)
    if with_q:
        cw = 2 * LANES
        for c in range(nh * HEAD_DIM // cw):
            qc = _dot(hb, wq_ref[:, c * cw:(c + 1) * cw])
            for half in range(2):
                y = _head_norm(qc[:, half * LANES:(half + 1) * LANES], qg_ref[...])
                if rope:
                    y = _rope(y, cos, sin_a, sin_b)
                yb = y.astype(BF)
                q_ref[4 * c + 2 * half] = yb[:, :HEAD_DIM]
                q_ref[4 * c + 2 * half + 1] = yb[:, HEAD_DIM:]


def _qkv_proj(x2, nb, g, sh, sc, wq, wkv, qg, kg, tables):
    m, d = x2.shape
    s = m // nb
    nkv = wkv.shape[1] // (2 * HEAD_DIM)
    with_q = wq is not None
    tm = min(s, 256)
    nt = s // tm
    shared = sh.shape[0] == 1
    mod_map = (lambda b, i: (0, 0, 0)) if shared else (lambda b, i: (b, 0, 0))
    row_spec = pl.BlockSpec((tm, d), lambda b, i: (b * nt + i, 0))
    vec_d = pl.BlockSpec((1, d), lambda b, i: (0, 0))
    mod_spec = pl.BlockSpec((None, 1, d), mod_map)
    vec_l = pl.BlockSpec((1, LANES), lambda b, i: (0, 0))
    tab_spec = pl.BlockSpec((tm, LANES), lambda b, i: (i, 0))
    kv_spec = pl.BlockSpec((None, nkv, tm, HEAD_DIM), lambda b, i: (b, 0, i, 0))
    kv_shape = jax.ShapeDtypeStruct((nb, nkv, s, HEAD_DIM), BF)
    if with_q:
        nh = wq.shape[1] // HEAD_DIM
        ins = [x2, g, sh, sc, wq, wkv, qg, kg, *tables]
        in_specs = [row_spec, vec_d, mod_spec, mod_spec,
                    pl.BlockSpec(wq.shape, lambda b, i: (0, 0)),
                    pl.BlockSpec(wkv.shape, lambda b, i: (0, 0)),
                    vec_l, vec_l, tab_spec, tab_spec, tab_spec]
        out_specs = [pl.BlockSpec((None, nh, tm, HEAD_DIM), lambda b, i: (b, 0, i, 0)), kv_spec, kv_spec]
        out_shape = [jax.ShapeDtypeStruct((nb, nh, s, HEAD_DIM), BF), kv_shape, kv_shape]
    else:
        nh = 0
        ins = [x2, g, sh, sc, wkv, kg]
        in_specs = [row_spec, vec_d, mod_spec, mod_spec,
                    pl.BlockSpec(wkv.shape, lambda b, i: (0, 0)), vec_l]
        out_specs = [kv_spec, kv_spec]
        out_shape = [kv_shape, kv_shape]
    return pl.pallas_call(
        functools.partial(_qkv_kernel, nh=nh, nkv=nkv, rope=with_q, with_q=with_q),
        grid=(nb, nt), in_specs=in_specs, out_specs=out_specs, out_shape=out_shape,
        compiler_params=_cparams("arbitrary", "arbitrary"),
    )(*ins)


def _attn_kernel(sink_ref, q_ref, kp_ref, kc_ref, kn_ref, vp_ref, vc_ref, vn_ref,
                 kx_ref, vx_ref, o_ref, *, nkv, nblk):
    i = pl.program_id(1)
    rows = GROUP * ATT_BLOCK
    qi = lax.broadcasted_iota(I32, (rows, ATT_BLOCK), 0) % ATT_BLOCK
    kj = lax.broadcasted_iota(I32, (rows, ATT_BLOCK), 1)
    prev_ok = (kj >= qi) & (i > 0)
    next_ok = (kj <= qi) & (i < nblk - 1)
    for h in range(nkv):
        q = q_ref[GROUP * h:GROUP * (h + 1)].reshape(rows, HEAD_DIM)
        s = jnp.concatenate([jnp.where(prev_ok, _dot_nt(q, kp_ref[h]), NEG_INF),
                             _dot_nt(q, kc_ref[h]),
                             jnp.where(next_ok, _dot_nt(q, kn_ref[h]), NEG_INF),
                             _dot_nt(q, kx_ref[h])], axis=1)
        sink = jnp.concatenate(
            [jnp.full((ATT_BLOCK, 1), sink_ref[GROUP * h + g] * LOG2E, F32) for g in range(GROUP)], axis=0)
        m = jnp.maximum(jnp.max(s, axis=-1, keepdims=True), sink)
        p = jnp.exp2(s - m)
        den = jnp.sum(p, axis=-1, keepdims=True) + jnp.exp2(sink - m)
        pb = p.astype(BF)
        w = ATT_BLOCK
        o = (_dot(pb[:, :w], vp_ref[h]) + _dot(pb[:, w:2 * w], vc_ref[h])
             + _dot(pb[:, 2 * w:3 * w], vn_ref[h]) + _dot(pb[:, 3 * w:], vx_ref[h]))
        o = (o / den).astype(BF)
        for g in range(GROUP):
            hh = GROUP * h + g
            o_ref[:, hh * HEAD_DIM:(hh + 1) * HEAD_DIM] = o[g * ATT_BLOCK:(g + 1) * ATT_BLOCK]


def _attention(q, k, v, kx, vx, sink):
    nb, nh, s, _ = q.shape
    nkv = k.shape[1]
    lc = kx.shape[2]
    nblk = s // ATT_BLOCK
    blk = lambda f: pl.BlockSpec((None, nkv, ATT_BLOCK, HEAD_DIM), f)
    prev = lambda b, i: (b, 0, jnp.maximum(i - 1, 0), 0)
    cur = lambda b, i: (b, 0, i, 0)
    nxt = lambda b, i: (b, 0, jnp.minimum(i + 1, nblk - 1), 0)
    ctx_spec = pl.BlockSpec((None, nkv, lc, HEAD_DIM), lambda b, i: (b, 0, 0, 0))
    return pl.pallas_call(
        functools.partial(_attn_kernel, nkv=nkv, nblk=nblk),
        grid=(nb, nblk),
        in_specs=[pl.BlockSpec(memory_space=pltpu.SMEM),
                  pl.BlockSpec((None, nh, ATT_BLOCK, HEAD_DIM), cur),
                  blk(prev), blk(cur), blk(nxt), blk(prev), blk(cur), blk(nxt), ctx_spec, ctx_spec],
        out_specs=pl.BlockSpec((ATT_BLOCK, nh * HEAD_DIM), lambda b, i: (b * nblk + i, 0)),
        out_shape=jax.ShapeDtypeStruct((nb * s, nh * HEAD_DIM), BF),
        compiler_params=_cparams("arbitrary", "arbitrary"),
    )(sink, q, k, k, k, v, v, v, kx, vx)


def _mm_res_kernel(a_ref, w_ref, res_ref, gate_ref, o_ref):
    o_ref[...] = res_ref[...] + gate_ref[...] * _dot(a_ref[...], w_ref[...])


def _mm_res(a, w, res, gate, nb):
    m, kd = a.shape
    n = w.shape[1]
    tm = min(m // nb, 512)
    per_b = m // nb // tm
    return pl.pallas_call(
        _mm_res_kernel,
        grid=(m // tm,),
        in_specs=[pl.BlockSpec((tm, kd), lambda i: (i, 0)),
                  pl.BlockSpec((kd, n), lambda i: (0, 0)),
                  pl.BlockSpec((tm, n), lambda i: (i, 0)),
                  pl.BlockSpec((None, 1, n), lambda i: (i // per_b, 0, 0))],
        out_specs=pl.BlockSpec((tm, n), lambda i: (i, 0)),
        out_shape=jax.ShapeDtypeStruct((m, n), F32),
        compiler_params=_cparams("arbitrary"),
    )(a, w, res, gate)


def _pw1_kernel(x_ref, g_ref, sh_ref, sc_ref, wa_ref, wg_ref, ba_ref, bg_ref, o_ref):
    hb = _norm_mod(x_ref[...], g_ref[...], sh_ref[...], sc_ref[...]).astype(BF)
    a = _dot(hb, wa_ref[...]) + ba_ref[...]
    gt = _dot(hb, wg_ref[...]) + bg_ref[...]
    o_ref[...] = a * jax.nn.sigmoid(gt)


def _pw1_glu(x2, nb, g, sh, sc, w, bias):
    m, d = x2.shape
    tm = min(m // nb, 512)
    per_b = m // nb // tm
    tn = min(d, 1024)
    nj = d // tn
    mod_spec = pl.BlockSpec((None, 1, d), lambda j, i: (i // per_b, 0, 0))
    return pl.pallas_call(
        _pw1_kernel,
        grid=(nj, m // tm),
        in_specs=[pl.BlockSpec((tm, d), lambda j, i: (i, 0)),
                  pl.BlockSpec((1, d), lambda j, i: (0, 0)), mod_spec, mod_spec,
                  pl.BlockSpec((d, tn), lambda j, i: (0, j)),
                  pl.BlockSpec((d, tn), lambda j, i: (0, nj + j)),
                  pl.BlockSpec((1, tn), lambda j, i: (0, j)),
                  pl.BlockSpec((1, tn), lambda j, i: (0, nj + j))],
        out_specs=pl.BlockSpec((tm, tn), lambda j, i: (i, j)),
        out_shape=jax.ShapeDtypeStruct((m, d), F32),
        compiler_params=_cparams("arbitrary", "arbitrary"),
    )(x2, g, sh, sc, w, w, bias, bias)


def _conv_kernel(up_ref, uc_ref, un_ref, wdw_ref, bdw_ref, lg_ref, lb_ref, w2_ref, b2_ref,
                 res_ref, gate_ref, o_ref, ext_ref, cv_ref, *, nt):
    i = pl.program_id(1)
    tm, d = uc_ref.shape
    zero = jnp.zeros((CONV_HALO, d), F32)
    ext_ref[0:CONV_HALO, :] = jnp.where(i > 0, up_ref[...], zero)
    ext_ref[CONV_HALO:CONV_HALO + tm, :] = uc_ref[...]
    ext_ref[CONV_HALO + tm:, :] = jnp.where(i < nt - 1, un_ref[...], zero)
    off = CONV_HALO - CONV_WIDTH // 2
    rb = min(tm, CONV_ROWS)
    span = rb + 2 * CONV_HALO
    for c in range(d // LANES):
        cols = slice(c * LANES, (c + 1) * LANES)
        wcol = wdw_ref[:, cols]
        for r0 in range(0, tm, rb):
            xin = ext_ref[r0:r0 + span, cols]
            shifted = {0: xin}
            acc = jnp.broadcast_to(bdw_ref[:, cols], (rb, LANES))
            for k in range(CONV_WIDTH):
                r, a = (off + k) % SUBLANES, (off + k) // SUBLANES
                if r not in shifted:
                    shifted[r] = pltpu.roll(xin, span - r, 0)
                acc = acc + wcol[k:k + 1] * shifted[r][SUBLANES * a:SUBLANES * a + rb]
            cv_ref[r0:r0 + rb, cols] = acc
    u = cv_ref[...]
    mu = jnp.mean(u, axis=-1, keepdims=True)
    var = jnp.mean(jnp.square(u - mu), axis=-1, keepdims=True)
    y = (u - mu) * lax.rsqrt(var + EPS) * lg_ref[...] + lb_ref[...]
    y = y * jax.nn.sigmoid(y)
    out = _dot(y.astype(BF), w2_ref[...]) + b2_ref[...]
    o_ref[...] = res_ref[...] + gate_ref[...] * out


def _conv_pw2(u, nb, wdw, bdw, lg, lb, w2, b2, res, gate):
    m, d = u.shape
    s = m // nb
    tm = min(s, 256)
    nt = s // tm
    hb = tm // CONV_HALO
    nhb = s // CONV_HALO
    vec = pl.BlockSpec((1, d), lambda b, i: (0, 0))
    tile = pl.BlockSpec((tm, d), lambda b, i: (b * nt + i, 0))
    return pl.pallas_call(
        functools.partial(_conv_kernel, nt=nt),
        grid=(nb, nt),
        in_specs=[pl.BlockSpec((CONV_HALO, d), lambda b, i: (b * nhb + jnp.maximum(i * hb - 1, 0), 0)),
                  tile,
                  pl.BlockSpec((CONV_HALO, d), lambda b, i: (b * nhb + jnp.minimum((i + 1) * hb, nhb - 1), 0)),
                  pl.BlockSpec((CONV_WIDTH, d), lambda b, i: (0, 0)),
                  vec, vec, vec,
                  pl.BlockSpec((d, d), lambda b, i: (0, 0)),
                  vec, tile,
                  pl.BlockSpec((None, 1, d), lambda b, i: (b, 0, 0))],
        out_specs=tile,
        out_shape=jax.ShapeDtypeStruct((m, d), F32),
        scratch_shapes=[pltpu.VMEM((tm + 2 * CONV_HALO, d), F32), pltpu.VMEM((tm, d), F32)],
        compiler_params=_cparams("arbitrary", "arbitrary"),
    )(u, u, u, wdw, bdw, lg, lb, w2, b2, res, gate)


def _router_kernel(x_ref, g_ref, sh_ref, sc_ref, wr_ref, xp_ref, aff_ref):
    hn = _norm_mod(x_ref[...], g_ref[...], sh_ref[...], sc_ref[...])
    h_hi, h_lo = _split_bf16(hn)
    w_hi, w_lo = _split_bf16(wr_ref[...])
    logits = _dot_nt(w_hi, h_hi) + _dot_nt(w_hi, h_lo) + _dot_nt(w_lo, h_hi)
    mx = jnp.max(logits, axis=0, keepdims=True)
    ex = jnp.exp(logits - mx)
    aff_ref[...] = ex / jnp.sum(ex, axis=0, keepdims=True)
    bits = lax.bitcast_convert_type(h_hi.astype(F32), U32)
    d2 = bits.shape[1] // 2
    xp_ref[...] = bits[:, :d2] | (bits[:, d2:] >> 16)


def _router(x2, nb, g, sh, sc, wr_t):
    m, d = x2.shape
    s = m // nb
    ne = wr_t.shape[0]
    tm = min(s, 512)
    nt = s // tm
    mod_spec = pl.BlockSpec((None, 1, d), lambda b, i: (b, 0, 0))
    return pl.pallas_call(
        _router_kernel,
        grid=(nb, nt),
        in_specs=[pl.BlockSpec((tm, d), lambda b, i: (b * nt + i, 0)),
                  pl.BlockSpec((1, d), lambda b, i: (0, 0)), mod_spec, mod_spec,
                  pl.BlockSpec((ne, d), lambda b, i: (0, 0))],
        out_specs=[pl.BlockSpec((tm, d // 2), lambda b, i: (b * nt + i, 0)),
                   pl.BlockSpec((None, ne, tm), lambda b, i: (b, 0, i))],
        out_shape=[jax.ShapeDtypeStruct((m, d // 2), U32),
                   jax.ShapeDtypeStruct((nb, ne, s), F32)],
        compiler_params=_cparams("arbitrary", "arbitrary"),
    )(x2, g, sh, sc, wr_t)


def _prefix_incl(mask_bf):
    ne, s = mask_bf.shape
    r = lax.broadcasted_iota(I32, (LANES, LANES), 0)
    c = lax.broadcasted_iota(I32, (LANES, LANES), 1)
    tri = jnp.where(r <= c, 1.0, 0.0).astype(BF)
    carry = jnp.zeros((ne, 1), F32)
    out = []
    for ch in range(s // LANES):
        inc = _dot(mask_bf[:, ch * LANES:(ch + 1) * LANES], tri) + carry
        out.append(inc)
        carry = inc[:, LANES - 1:LANES]
    return jnp.concatenate(out, axis=1)


def _topk_kernel(aff_ref, pos_ref, gate_ref, cnt_ref, idx_ref, pos_scr, *, cap):
    aff = aff_ref[...]
    ne, s = aff.shape
    v = lax.bitcast_convert_type(aff, I32)

    def bit_step(it, prefix):
        cand = prefix | jnp.left_shift(jnp.int32(1), 30 - it)
        cnt = jnp.sum(jnp.where(v >= cand, 1.0, 0.0), axis=1, keepdims=True)
        return jnp.where(cnt >= cap, cand, prefix)

    thr = lax.fori_loop(0, 31, bit_step, jnp.zeros((ne, 1), I32))
    gt = v > thr
    eq = v == thr
    need = cap - jnp.sum(jnp.where(gt, 1.0, 0.0), axis=1, keepdims=True)
    eq_bf = jnp.where(eq, 1.0, 0.0).astype(BF)
    eq_rank = _prefix_incl(eq_bf) - eq_bf.astype(F32)
    sel = gt | (eq & (eq_rank < need))
    sel_bf = jnp.where(sel, 1.0, 0.0).astype(BF)
    incl = _prefix_incl(sel_bf)
    pos = jnp.where(sel, incl - 1.0, -1.0).astype(I32)
    pos_ref[...] = pos
    gate_ref[...] = jnp.where(sel, aff, 0.0)
    t_i = lax.broadcasted_iota(I32, (s, LANES), 0)
    j_i = lax.broadcasted_iota(I32, (s, LANES), 1)
    before = jnp.where(t_i < j_i * LANES, 1.0, 0.0).astype(BF)
    cnt_ref[...] = _dot(sel_bf, before).astype(I32)
    pos_scr[...] = pos
    t_row = lax.broadcasted_iota(I32, (8, s), 1)
    r_row = lax.broadcasted_iota(I32, (8, s), 0)
    tval = jnp.where(r_row == 0, t_row // 64, jnp.where(r_row == 1, t_row % 64, 0)).astype(F32).astype(BF)
    slot = lax.broadcasted_iota(I32, (cap, 1), 0)

    def expert_step(e, carry):
        onehot = jnp.where(pos_scr[pl.ds(e, 1), :] == slot, 1.0, 0.0).astype(BF)
        res = _dot_nt(tval, onehot)
        idx_ref[pl.ds(e, 1), :] = (res[0:1] * 64.0 + res[1:2]).astype(I32)
        return carry

    lax.fori_loop(0, ne, expert_step, 0)


def _topk(aff_t, cap):
    nb, ne, s = aff_t.shape
    blk = pl.BlockSpec((None, ne, s), lambda b: (b, 0, 0))
    return pl.pallas_call(
        functools.partial(_topk_kernel, cap=cap),
        grid=(nb,),
        in_specs=[blk],
        out_specs=[blk, blk,
                   pl.BlockSpec((None, ne, LANES), lambda b: (b, 0, 0)),
                   pl.BlockSpec((None, ne, cap), lambda b: (b, 0, 0))],
        out_shape=[jax.ShapeDtypeStruct((nb, ne, s), I32),
                   jax.ShapeDtypeStruct((nb, ne, s), F32),
                   jax.ShapeDtypeStruct((nb, ne, LANES), I32),
                   jax.ShapeDtypeStruct((nb, ne, cap), I32)],
        scratch_shapes=[pltpu.VMEM((ne, s), I32)],
        compiler_params=_cparams("arbitrary"),
    )(aff_t)


def _row_copy(src_ref, src_row, dst_ref, dst_row, sem):
    return pltpu.make_async_copy(src_ref.at[pl.ds(src_row, 1), :], dst_ref.at[pl.ds(dst_row, 1), :], sem)


def _expert_up_kernel(cur_ref, nxt_ref, xp_ref, wg_ref, wu_ref, o_ref, raw_ref, xb_ref, sem, *, ne, n, nj):
    e = pl.program_id(0)
    j = pl.program_id(1)
    slot = e % 2
    per = n // nj

    def issue(idx_ref, dst, lo, cnt):
        def body(c, carry):
            _row_copy(xp_ref, idx_ref[0, c], raw_ref.at[dst], c, sem.at[dst]).start()
            return carry

        lax.fori_loop(lo, lo + cnt, body, 0)

    @pl.when((e == 0) & (j == 0))
    def _():
        issue(cur_ref, 0, 0, n)

    @pl.when(e + 1 < ne)
    def _():
        issue(nxt_ref, 1 - slot, j * per, per)

    @pl.when(j == 0)
    def _():
        def drain(c, carry):
            _row_copy(xp_ref, 0, raw_ref.at[slot], c, sem.at[slot]).wait()
            return carry

        lax.fori_loop(0, n, drain, 0)
        w = raw_ref[slot]
        d2 = w.shape[1]
        xb_ref[:, :d2] = lax.bitcast_convert_type(w & jnp.uint32(0xFFFF0000), F32).astype(BF)
        xb_ref[:, d2:] = lax.bitcast_convert_type(w << 16, F32).astype(BF)

    x = xb_ref[...]
    g = _dot(x, wg_ref[...].astype(BF))
    u = _dot(x, wu_ref[...].astype(BF))
    o_ref[...] = (g * jax.nn.sigmoid(g) * u).astype(BF)


def _expert_up(xp, rows, w_gate, w_up, layer):
    _, ne, d, f = w_gate.shape
    n = rows.shape[2]
    tf = min(f, 256)
    nj = f // tf
    w_spec = pl.BlockSpec((None, None, d, tf), lambda e, j: (layer, e, 0, j))
    idx_spec = lambda f_: pl.BlockSpec((None, 1, n), f_, memory_space=pltpu.SMEM)
    return pl.pallas_call(
        functools.partial(_expert_up_kernel, ne=ne, n=n, nj=nj),
        grid=(ne, nj),
        in_specs=[idx_spec(lambda e, j: (e, 0, 0)),
                  idx_spec(lambda e, j: (jnp.minimum(e + 1, ne - 1), 0, 0)),
                  pl.BlockSpec(memory_space=pl.ANY), w_spec, w_spec],
        out_specs=pl.BlockSpec((None, n, tf), lambda e, j: (e, 0, j)),
        out_shape=jax.ShapeDtypeStruct((ne, n, f), BF),
        scratch_shapes=[pltpu.VMEM((2, n, d // 2), U32), pltpu.VMEM((n, d), BF),
                        pltpu.SemaphoreType.DMA((2,))],
        compiler_params=_cparams("arbitrary", "arbitrary"),
    )(rows, rows, xp, w_gate, w_up)


def _expert_down_kernel(h_ref, w_ref, o_ref):
    o_ref[...] = _dot(h_ref[...], w_ref[...].astype(BF))


def _expert_down(hid, w_down, layer):
    ne, n, f = hid.shape
    d = w_down.shape[3]
    tn = min(d, 512)
    return pl.pallas_call(
        _expert_down_kernel,
        grid=(ne, d // tn),
        in_specs=[pl.BlockSpec((None, n, f), lambda e, j: (e, 0, 0)),
                  pl.BlockSpec((None, None, f, tn), lambda e, j: (layer, e, 0, j))],
        out_specs=pl.BlockSpec((None, n, tn), lambda e, j: (e, 0, j)),
        out_shape=jax.ShapeDtypeStruct((ne, n, d), F32),
        compiler_params=_cparams("arbitrary", "arbitrary"),
    )(hid, w_down)


COMBINE_TOKENS = 128
COMBINE_CHUNK = 256
SLAB_ROWS = (128, 64, 32, 16, 8)
COMBINE_SLOTS_PER_EXPERT = COMBINE_TOKENS + SUBLANES


def _slab_copy(y_ref, src_row, buf_ref, dst_row, rows, sem):
    return pltpu.make_async_copy(y_ref.at[pl.ds(src_row, rows), :], buf_ref.at[pl.ds(dst_row, rows), :], sem)


def _combine_kernel(cnt_ref, y_ref, pos_ref, gt_ref, res_ref, g2_ref, o_ref, buf_ref, acc_ref, sem,
                    *, ne, n, cap, nbound):
    b = pl.program_id(0)
    t = pl.program_id(1)

    @pl.when((b == 0) & (t == 0))
    def _():
        buf_ref[...] = jnp.zeros_like(buf_ref)

    shift = []
    issued = []
    total = jnp.int32(0)
    for e in range(ne):
        base = (b * ne + e) * nbound + t
        c0 = cnt_ref[base]
        c1 = cnt_ref[base + 1]
        a0 = (c0 // SUBLANES) * SUBLANES
        run = jnp.where(c1 > c0, (c1 + SUBLANES - 1) // SUBLANES * SUBLANES - a0, 0)
        row0 = e * n + b * cap + a0
        done = jnp.int32(0)
        for rows in SLAB_ROWS:
            take = (run & rows) != 0
            src = pl.multiple_of(row0 + done, SUBLANES)
            dst = pl.multiple_of(total + done, SUBLANES)

            @pl.when(take)
            def _(src=src, dst=dst, rows=rows):
                _slab_copy(y_ref, src, buf_ref, dst, rows, sem).start()

            issued.append((take, rows))
            done = done + jnp.where(take, rows, 0)
        shift.append(total - a0)
        total = total + run

    for take, rows in issued:
        @pl.when(take)
        def _(rows=rows):
            _slab_copy(y_ref, 0, buf_ref, 0, rows, sem).wait()

    acc_ref[...] = jnp.zeros_like(acc_ref)
    pos = pos_ref[...]
    gts = gt_ref[...]

    def chunk(ci, carry):
        start = pl.multiple_of(ci * COMBINE_CHUNK, COMBINE_CHUNK)
        lane = lax.broadcasted_iota(I32, (COMBINE_TOKENS, COMBINE_CHUNK), 1) + start
        gmat = jnp.zeros((COMBINE_TOKENS, COMBINE_CHUNK), F32)
        for e in range(ne):
            pe = pos[:, e:e + 1]
            gmat = jnp.where((lane == pe + shift[e]) & (pe >= 0), gts[:, e:e + 1], gmat)
        live = lax.broadcasted_iota(I32, (COMBINE_CHUNK, 1), 0) + start < total
        rows = jnp.where(live, buf_ref[pl.ds(start, COMBINE_CHUNK), :], 0.0)
        g_hi, g_lo = _split_bf16(gmat)
        r_hi, r_lo = _split_bf16(rows)
        acc_ref[...] += _dot(g_hi, r_hi) + _dot(g_hi, r_lo) + _dot(g_lo, r_hi)
        return carry

    lax.fori_loop(0, (total + COMBINE_CHUNK - 1) // COMBINE_CHUNK, chunk, 0)
    o_ref[...] = res_ref[...] + g2_ref[...] * acc_ref[...]


def _combine(cnt, y, pos_t, gate_t, res, g2, nb, cap):
    m, d = res.shape
    s = m // nb
    ne = pos_t.shape[1]
    n = nb * cap
    nt = s // COMBINE_TOKENS
    nbound = nt + 1
    tok = lambda b, t, c: (b * nt + t, 0)
    grid_spec = pltpu.PrefetchScalarGridSpec(
        num_scalar_prefetch=1,
        grid=(nb, nt),
        in_specs=[pl.BlockSpec(memory_space=pl.ANY),
                  pl.BlockSpec((COMBINE_TOKENS, ne), tok),
                  pl.BlockSpec((COMBINE_TOKENS, ne), tok),
                  pl.BlockSpec((COMBINE_TOKENS, d), tok),
                  pl.BlockSpec((None, 1, d), lambda b, t, c: (b, 0, 0))],
        out_specs=pl.BlockSpec((COMBINE_TOKENS, d), tok),
        scratch_shapes=[pltpu.VMEM((pl.cdiv(ne * COMBINE_SLOTS_PER_EXPERT, COMBINE_CHUNK) * COMBINE_CHUNK, d), F32),
                        pltpu.VMEM((COMBINE_TOKENS, d), F32),
                        pltpu.SemaphoreType.DMA(())],
    )
    return pl.pallas_call(
        functools.partial(_combine_kernel, ne=ne, n=n, cap=cap, nbound=nbound),
        grid_spec=grid_spec,
        out_shape=jax.ShapeDtypeStruct((m, d), F32),
        compiler_params=_cparams("arbitrary", "arbitrary"),
    )(cnt[:, :, :nbound].reshape(-1), y, pos_t, gate_t, res, g2)


def _moe(h2, nb, g, sh, sc, gate2, w_router, w_gate, w_up, w_down, layer):
    m, d = h2.shape
    s = m // nb
    ne = w_router.shape[1]
    cap = EC_FACTOR * s // ne
    xp, aff_t = _router(h2, nb, g, sh, sc, w_router.T)
    pos, gate, cnt, idx = _topk(aff_t, cap)
    rows = idx + (jnp.arange(nb, dtype=I32) * s)[:, None, None]
    rows = jnp.swapaxes(rows, 0, 1).reshape(ne, 1, nb * cap)
    hid = _expert_up(xp, rows, w_gate, w_up, layer)
    y = _expert_down(hid, w_down, layer).reshape(ne * nb * cap, d)
    pos_t = jnp.swapaxes(pos, 1, 2).reshape(m, ne)
    gate_t = jnp.swapaxes(gate, 1, 2).reshape(m, ne)
    return _combine(cnt, y, pos_t, gate_t, h2, gate2, nb, cap)


def _rope_tables(s):
    rows = s // GRID_W
    r, col = jnp.meshgrid(jnp.arange(rows, dtype=F32), jnp.arange(GRID_W, dtype=F32), indexing="ij")
    r, col = r.reshape(-1), col.reshape(-1)
    n_pairs = HEAD_DIM // 4
    inv = ROPE_BASE ** (-jnp.arange(n_pairs, dtype=F32) / n_pairs)
    ang_r = r[:, None] * inv[None, :]
    ang_c = col[:, None] * inv[None, :]
    ang = jnp.concatenate([ang_r, ang_r, ang_c, ang_c], axis=-1)
    cos, sin = jnp.cos(ang), jnp.sin(ang)
    first = (jnp.arange(HEAD_DIM) % (HEAD_DIM // 2)) < HEAD_DIM // 4
    sin_a = jnp.where(first, -sin, 0.0)
    sin_b = jnp.where(first, 0.0, sin)
    two = lambda a: jnp.concatenate([a, a], axis=-1)
    return two(cos), two(sin_a), two(sin_b)


def kernel(x, c, ctx, c_ctx, ada_w, ada_b, norm1_g, norm2_g, attn_w_q, attn_w_kv, attn_w_o, attn_q_gain, attn_k_gain, attn_sink, conv_w_pw1, conv_b_pw1, conv_w_dw, conv_b_dw, conv_ln_g, conv_ln_b, conv_w_pw2, conv_b_pw2, moe_router, moe_w_gate, moe_w_up, moe_w_down):
    nb, s, d = x.shape
    lc = ctx.shape[1]
    depth = ada_w.shape[0]
    assert depth == 2, "layer 0 attention, layer 1 convolution; the context stream is never updated"
    m = nb * s

    cin = jnp.concatenate([c, c_ctx[None, :], jnp.zeros((8 - nb - 1, d), F32)], axis=0)
    mod = _ada(cin, ada_w, ada_b)

    def mods(layer, row0, nrow):
        return [mod[layer, row0:row0 + nrow, k * d:(k + 1) * d].reshape(nrow, 1, d) for k in range(6)]

    h = x.reshape(m, d)
    two = lambda a: jnp.concatenate([a, a], axis=-1).reshape(1, LANES)

    sh1, sc1, g1, sh2, sc2, g2 = mods(0, 0, nb)
    sh1c, sc1c = mods(0, nb, 1)[:2]
    n1 = norm1_g[0].reshape(1, d)
    wq = attn_w_q[0].astype(BF)
    wkv = attn_w_kv[0].astype(BF)
    wo = attn_w_o[0].astype(BF)
    q_gain = two(attn_q_gain[0]) * (HEAD_DIM ** -0.5 * LOG2E)
    k_gain = two(attn_k_gain[0])
    q, k, v = _qkv_proj(h, nb, n1, sh1, sc1, wq, wkv, q_gain, k_gain, _rope_tables(s))
    kx, vx = _qkv_proj(ctx.reshape(nb * lc, d), nb, n1, sh1c, sc1c, None, wkv, None, k_gain, None)
    att = _attention(q, k, v, kx, vx, attn_sink[0])
    h = _mm_res(att, wo, h, g1, nb)
    h = _moe(h, nb, norm2_g[0].reshape(1, d), sh2, sc2, g2,
             moe_router[0], moe_w_gate, moe_w_up, moe_w_down, 0)

    sh1, sc1, g1, sh2, sc2, g2 = mods(1, 0, nb)
    u = _pw1_glu(h, nb, norm1_g[1].reshape(1, d), sh1, sc1,
                 conv_w_pw1[0].astype(BF), conv_b_pw1[0].reshape(1, 2 * d))
    h = _conv_pw2(u, nb, conv_w_dw[0], conv_b_dw[0].reshape(1, d), conv_ln_g[0].reshape(1, d),
                  conv_ln_b[0].reshape(1, d), conv_w_pw2[0].astype(BF), conv_b_pw2[0].reshape(1, d), h, g1)
    h = _moe(h, nb, norm2_g[1].reshape(1, d), sh2, sc2, g2,
             moe_router[1], moe_w_gate, moe_w_up, moe_w_down, 1)
    return h.reshape(nb, s, d)
```

```python
import functools

import jax
import jax.numpy as jnp
from jax import lax
from jax.experimental import pallas as pl
from jax.experimental.pallas import tpu as pltpu

F32 = jnp.float32
BF = jnp.bfloat16
I32 = jnp.int32
U32 = jnp.uint32

HEAD_DIM = 64
GROUP = 4
GRID_W = 64
ROPE_BASE = 10000.0
ATT_BLOCK = 128
CONV_WIDTH = 31
CONV_HALO = 16
CONV_ROWS = 64
SUBLANES = 8
N_EXPERTS = 16
EC_FACTOR = 2
EPS = 1e-6
NEG_INF = -1e30
LOG2E = 1.4426950408889634
LANES = 128
V7X_VMEM_LIMIT = 56 * 1024 * 1024


def _cparams(*sem):
    return pltpu.CompilerParams(dimension_semantics=sem, vmem_limit_bytes=V7X_VMEM_LIMIT)


def _dot(a, b):
    return jnp.dot(a, b, preferred_element_type=F32)


def _dot_nt(a, b):
    return lax.dot_general(a, b, (((1,), (1,)), ((), ())), preferred_element_type=F32)


def _split_bf16(x):
    hi = x.astype(BF)
    lo = (x - hi.astype(F32)).astype(BF)
    return hi, lo


def _norm_mod(x, g, sh, sc):
    ms = jnp.mean(x * x, axis=-1, keepdims=True)
    return (x * lax.rsqrt(ms + EPS) * g) * (1.0 + sc) + sh


def _ada_kernel(c_ref, w_ref, b_ref, o_ref):
    c = c_ref[...]
    s = c * jax.nn.sigmoid(c)
    o_ref[...] = _dot(s.astype(BF), w_ref[...].astype(BF)) + b_ref[...]


def _ada(cin, ada_w, ada_b):
    depth, d, n = ada_w.shape
    rows = cin.shape[0]
    tn = min(n, 1024)
    return pl.pallas_call(
        _ada_kernel,
        grid=(depth, n // tn),
        in_specs=[pl.BlockSpec((rows, d), lambda l, j: (0, 0)),
                  pl.BlockSpec((None, d, tn), lambda l, j: (l, 0, j)),
                  pl.BlockSpec((None, 1, tn), lambda l, j: (l, 0, j))],
        out_specs=pl.BlockSpec((None, rows, tn), lambda l, j: (l, 0, j)),
        out_shape=jax.ShapeDtypeStruct((depth, rows, n), F32),
        compiler_params=_cparams("arbitrary", "arbitrary"),
    )(cin, ada_w, ada_b.reshape(depth, 1, n))


def _head_norm(y, gain):
    r = lax.broadcasted_iota(I32, (LANES, LANES), 0) // HEAD_DIM
    c = lax.broadcasted_iota(I32, (LANES, LANES), 1) // HEAD_DIM
    bd = jnp.where(r == c, 1.0 / HEAD_DIM, 0.0).astype(BF)
    hi, lo = _split_bf16(y * y)
    ms = _dot(hi, bd) + _dot(lo, bd)
    return y * lax.rsqrt(ms + EPS) * gain


def _rope(y, cos, sin_a, sin_b):
    return (y * cos + pltpu.roll(y, LANES - HEAD_DIM // 4, 1) * sin_a
            + pltpu.roll(y, HEAD_DIM // 4, 1) * sin_b)


def _qkv_kernel(*refs, nh, nkv, rope, with_q):
    if with_q:
        (x_ref, g_ref, sh_ref, sc_ref, wq_ref, wkv_ref, qg_ref, kg_ref,
         cos_ref, sa_ref, sb_ref, q_ref, k_ref, v_ref) = refs
    else:
        (x_ref, g_ref, sh_ref, sc_ref, wkv_ref, kg_ref, k_ref, v_ref) = refs
    hb = _norm_mod(x_ref[...], g_ref[...], sh_ref[...], sc_ref[...]).astype(BF)
    if rope:
        cos, sin_a, sin_b = cos_ref[...], sa_ref[...], sb_ref[...]
    kvd = nkv * HEAD_DIM
    kv = _dot(hb, wkv_ref[...])
    for c in range(kvd // LANES):
        y = _head_norm(kv[:, c * LANES:(c + 1) * LANES], kg_ref[...])
        if rope:
            y = _rope(y, cos, sin_a, sin_b)
        yb = y.astype(BF)
        k_ref[2 * c] = yb[:, :HEAD_DIM]
        k_ref[2 * c + 1] = yb[:, HEAD_DIM:]
    for h in range(nkv):
        v_ref[h] = kv[:, kvd + h * HEAD_DIM:kvd + (h + 1) * HEAD_DIM].astype(BF)
    if with_q:
        cw = 2 * LANES
        for c in range(nh * HEAD_DIM // cw):
            qc = _dot(hb, wq_ref[:, c * cw:(c + 1) * cw])
            for half in range(2):
                y = _head_norm(qc[:, half * LANES:(half + 1) * LANES], qg_ref[...])
                if rope:
                    y = _rope(y, cos, sin_a, sin_b)
                yb = y.astype(BF)
                q_ref[4 * c + 2 * half] = yb[:, :HEAD_DIM]
                q_ref[4 * c + 2 * half + 1] = yb[:, HEAD_DIM:]


def _qkv_proj(x2, nb, g, sh, sc, wq, wkv, qg, kg, tables):
    m, d = x2.shape
    s = m // nb
    nkv = wkv.shape[1] // (2 * HEAD_DIM)
    with_q = wq is not None
    tm = min(s, 256)
    nt = s // tm
    shared = sh.shape[0] == 1
    mod_map = (lambda b, i: (0, 0, 0)) if shared else (lambda b, i: (b, 0, 0))
    row_spec = pl.BlockSpec((tm, d), lambda b, i: (b * nt + i, 0))
    vec_d = pl.BlockSpec((1, d), lambda b, i: (0, 0))
    mod_spec = pl.BlockSpec((None, 1, d), mod_map)
    vec_l = pl.BlockSpec((1, LANES), lambda b, i: (0, 0))
    tab_spec = pl.BlockSpec((tm, LANES), lambda b, i: (i, 0))
    kv_spec = pl.BlockSpec((None, nkv, tm, HEAD_DIM), lambda b, i: (b, 0, i, 0))
    kv_shape = jax.ShapeDtypeStruct((nb, nkv, s, HEAD_DIM), BF)
    if with_q:
        nh = wq.shape[1] // HEAD_DIM
        ins = [x2, g, sh, sc, wq, wkv, qg, kg, *tables]
        in_specs = [row_spec, vec_d, mod_spec, mod_spec,
                    pl.BlockSpec(wq.shape, lambda b, i: (0, 0)),
                    pl.BlockSpec(wkv.shape, lambda b, i: (0, 0)),
                    vec_l, vec_l, tab_spec, tab_spec, tab_spec]
        out_specs = [pl.BlockSpec((None, nh, tm, HEAD_DIM), lambda b, i: (b, 0, i, 0)), kv_spec, kv_spec]
        out_shape = [jax.ShapeDtypeStruct((nb, nh, s, HEAD_DIM), BF), kv_shape, kv_shape]
    else:
        nh = 0
        ins = [x2, g, sh, sc, wkv, kg]
        in_specs = [row_spec, vec_d, mod_spec, mod_spec,
                    pl.BlockSpec(wkv.shape, lambda b, i: (0, 0)), vec_l]
        out_specs = [kv_spec, kv_spec]
        out_shape = [kv_shape, kv_shape]
    return pl.pallas_call(
        functools.partial(_qkv_kernel, nh=nh, nkv=nkv, rope=with_q, with_q=with_q),
        grid=(nb, nt), in_specs=in_specs, out_specs=out_specs, out_shape=out_shape,
        compiler_params=_cparams("arbitrary", "arbitrary"),
    )(*ins)


def _attn_kernel(sink_ref, q_ref, kp_ref, kc_ref, kn_ref, vp_ref, vc_ref, vn_ref,
                 kx_ref, vx_ref, o_ref, *, nkv, nblk):
    i = pl.program_id(1)
    rows = GROUP * ATT_BLOCK
    qi = lax.broadcasted_iota(I32, (rows, ATT_BLOCK), 0) % ATT_BLOCK
    kj = lax.broadcasted_iota(I32, (rows, ATT_BLOCK), 1)
    prev_ok = (kj >= qi) & (i > 0)
    next_ok = (kj <= qi) & (i < nblk - 1)
    for h in range(nkv):
        q = q_ref[GROUP * h:GROUP * (h + 1)].reshape(rows, HEAD_DIM)
        s = jnp.concatenate([jnp.where(prev_ok, _dot_nt(q, kp_ref[h]), NEG_INF),
                             _dot_nt(q, kc_ref[h]),
                             jnp.where(next_ok, _dot_nt(q, kn_ref[h]), NEG_INF),
                             _dot_nt(q, kx_ref[h])], axis=1)
        sink = jnp.concatenate(
            [jnp.full((ATT_BLOCK, 1), sink_ref[GROUP * h + g] * LOG2E, F32) for g in range(GROUP)], axis=0)
        m = jnp.maximum(jnp.max(s, axis=-1, keepdims=True), sink)
        p = jnp.exp2(s - m)
        den = jnp.sum(p, axis=-1, keepdims=True) + jnp.exp2(sink - m)
        pb = p.astype(BF)
        w = ATT_BLOCK
        o = (_dot(pb[:, :w], vp_ref[h]) + _dot(pb[:, w:2 * w], vc_ref[h])
             + _dot(pb[:, 2 * w:3 * w], vn_ref[h]) + _dot(pb[:, 3 * w:], vx_ref[h]))
        o = (o / den).astype(BF)
        for g in range(GROUP):
            hh = GROUP * h + g
            o_ref[:, hh * HEAD_DIM:(hh + 1) * HEAD_DIM] = o[g * ATT_BLOCK:(g + 1) * ATT_BLOCK]


def _attention(q, k, v, kx, vx, sink):
    nb, nh, s, _ = q.shape
    nkv = k.shape[1]
    lc = kx.shape[2]
    nblk = s // ATT_BLOCK
    blk = lambda f: pl.BlockSpec((None, nkv, ATT_BLOCK, HEAD_DIM), f)
    prev = lambda b, i: (b, 0, jnp.maximum(i - 1, 0), 0)
    cur = lambda b, i: (b, 0, i, 0)
    nxt = lambda b, i: (b, 0, jnp.minimum(i + 1, nblk - 1), 0)
    ctx_spec = pl.BlockSpec((None, nkv, lc, HEAD_DIM), lambda b, i: (b, 0, 0, 0))
    return pl.pallas_call(
        functools.partial(_attn_kernel, nkv=nkv, nblk=nblk),
        grid=(nb, nblk),
        in_specs=[pl.BlockSpec(memory_space=pltpu.SMEM),
                  pl.BlockSpec((None, nh, ATT_BLOCK, HEAD_DIM), cur),
                  blk(prev), blk(cur), blk(nxt), blk(prev), blk(cur), blk(nxt), ctx_spec, ctx_spec],
        out_specs=pl.BlockSpec((ATT_BLOCK, nh * HEAD_DIM), lambda b, i: (b * nblk + i, 0)),
        out_shape=jax.ShapeDtypeStruct((nb * s, nh * HEAD_DIM), BF),
        compiler_params=_cparams("arbitrary", "arbitrary"),
    )(sink, q, k, k, k, v, v, v, kx, vx)


def _mm_res_kernel(a_ref, w_ref, res_ref, gate_ref, o_ref):
    o_ref[...] = res_ref[...] + gate_ref[...] * _dot(a_ref[...], w_ref[...])


def _mm_res(a, w, res, gate, nb):
    m, kd = a.shape
    n = w.shape[1]
    tm = min(m // nb, 512)
    per_b = m // nb // tm
    return pl.pallas_call(
        _mm_res_kernel,
        grid=(m // tm,),
        in_specs=[pl.BlockSpec((tm, kd), lambda i: (i, 0)),
                  pl.BlockSpec((kd, n), lambda i: (0, 0)),
                  pl.BlockSpec((tm, n), lambda i: (i, 0)),
                  pl.BlockSpec((None, 1, n), lambda i: (i // per_b, 0, 0))],
        out_specs=pl.BlockSpec((tm, n), lambda i: (i, 0)),
        out_shape=jax.ShapeDtypeStruct((m, n), F32),
        compiler_params=_cparams("arbitrary"),
    )(a, w, res, gate)


def _pw1_kernel(x_ref, g_ref, sh_ref, sc_ref, wa_ref, wg_ref, ba_ref, bg_ref, o_ref):
    hb = _norm_mod(x_ref[...], g_ref[...], sh_ref[...], sc_ref[...]).astype(BF)
    a = _dot(hb, wa_ref[...]) + ba_ref[...]
    gt = _dot(hb, wg_ref[...]) + bg_ref[...]
    o_ref[...] = a * jax.nn.sigmoid(gt)


def _pw1_glu(x2, nb, g, sh, sc, w, bias):
    m, d = x2.shape
    tm = min(m // nb, 512)
    per_b = m // nb // tm
    tn = min(d, 1024)
    nj = d // tn
    mod_spec = pl.BlockSpec((None, 1, d), lambda j, i: (i // per_b, 0, 0))
    return pl.pallas_call(
        _pw1_kernel,
        grid=(nj, m // tm),
        in_specs=[pl.BlockSpec((tm, d), lambda j, i: (i, 0)),
                  pl.BlockSpec((1, d), lambda j, i: (0, 0)), mod_spec, mod_spec,
                  pl.BlockSpec((d, tn), lambda j, i: (0, j)),
                  pl.BlockSpec((d, tn), lambda j, i: (0, nj + j)),
                  pl.BlockSpec((1, tn), lambda j, i: (0, j)),
                  pl.BlockSpec((1, tn), lambda j, i: (0, nj + j))],
        out_specs=pl.BlockSpec((tm, tn), lambda j, i: (i, j)),
        out_shape=jax.ShapeDtypeStruct((m, d), F32),
        compiler_params=_cparams("arbitrary", "arbitrary"),
    )(x2, g, sh, sc, w, w, bias, bias)


def _conv_kernel(up_ref, uc_ref, un_ref, wdw_ref, bdw_ref, lg_ref, lb_ref, w2_ref, b2_ref,
                 res_ref, gate_ref, o_ref, ext_ref, cv_ref, *, nt):
    i = pl.program_id(1)
    tm, d = uc_ref.shape
    zero = jnp.zeros((CONV_HALO, d), F32)
    ext_ref[0:CONV_HALO, :] = jnp.where(i > 0, up_ref[...], zero)
    ext_ref[CONV_HALO:CONV_HALO + tm, :] = uc_ref[...]
    ext_ref[CONV_HALO + tm:, :] = jnp.where(i < nt - 1, un_ref[...], zero)
    off = CONV_HALO - CONV_WIDTH // 2
    rb = min(tm, CONV_ROWS)
    span = rb + 2 * CONV_HALO
    for c in range(d // LANES):
        cols = slice(c * LANES, (c + 1) * LANES)
        wcol = wdw_ref[:, cols]
        for r0 in range(0, tm, rb):
            xin = ext_ref[r0:r0 + span, cols]
            shifted = {0: xin}
            acc = jnp.broadcast_to(bdw_ref[:, cols], (rb, LANES))
            for k in range(CONV_WIDTH):
                r, a = (off + k) % SUBLANES, (off + k) // SUBLANES
                if r not in shifted:
                    shifted[r] = pltpu.roll(xin, span - r, 0)
                acc = acc + wcol[k:k + 1] * shifted[r][SUBLANES * a:SUBLANES * a + rb]
            cv_ref[r0:r0 + rb, cols] = acc
    u = cv_ref[...]
    mu = jnp.mean(u, axis=-1, keepdims=True)
    var = jnp.mean(jnp.square(u - mu), axis=-1, keepdims=True)
    y = (u - mu) * lax.rsqrt(var + EPS) * lg_ref[...] + lb_ref[...]
    y = y * jax.nn.sigmoid(y)
    out = _dot(y.astype(BF), w2_ref[...]) + b2_ref[...]
    o_ref[...] = res_ref[...] + gate_ref[...] * out


def _conv_pw2(u, nb, wdw, bdw, lg, lb, w2, b2, res, gate):
    m, d = u.shape
    s = m // nb
    tm = min(s, 256)
    nt = s // tm
    hb = tm // CONV_HALO
    nhb = s // CONV_HALO
    vec = pl.BlockSpec((1, d), lambda b, i: (0, 0))
    tile = pl.BlockSpec((tm, d), lambda b, i: (b * nt + i, 0))
    return pl.pallas_call(
        functools.partial(_conv_kernel, nt=nt),
        grid=(nb, nt),
        in_specs=[pl.BlockSpec((CONV_HALO, d), lambda b, i: (b * nhb + jnp.maximum(i * hb - 1, 0), 0)),
                  tile,
                  pl.BlockSpec((CONV_HALO, d), lambda b, i: (b * nhb + jnp.minimum((i + 1) * hb, nhb - 1), 0)),
                  pl.BlockSpec((CONV_WIDTH, d), lambda b, i: (0, 0)),
                  vec, vec, vec,
                  pl.BlockSpec((d, d), lambda b, i: (0, 0)),
                  vec, tile,
                  pl.BlockSpec((None, 1, d), lambda b, i: (b, 0, 0))],
        out_specs=tile,
        out_shape=jax.ShapeDtypeStruct((m, d), F32),
        scratch_shapes=[pltpu.VMEM((tm + 2 * CONV_HALO, d), F32), pltpu.VMEM((tm, d), F32)],
        compiler_params=_cparams("arbitrary", "arbitrary"),
    )(u, u, u, wdw, bdw, lg, lb, w2, b2, res, gate)


def _router_kernel(x_ref, g_ref, sh_ref, sc_ref, wr_ref, xp_ref, aff_ref):
    hn = _norm_mod(x_ref[...], g_ref[...], sh_ref[...], sc_ref[...])
    h_hi, h_lo = _split_bf16(hn)
    w_hi, w_lo = _split_bf16(wr_ref[...])
    logits = _dot_nt(w_hi, h_hi) + _dot_nt(w_hi, h_lo) + _dot_nt(w_lo, h_hi)
    mx = jnp.max(logits, axis=0, keepdims=True)
    ex = jnp.exp(logits - mx)
    aff_ref[...] = ex / jnp.sum(ex, axis=0, keepdims=True)
    bits = lax.bitcast_convert_type(h_hi.astype(F32), U32)
    tm = bits.shape[0]
    d2 = bits.shape[1] // 2
    word = bits[:, :d2] | (bits[:, d2:] >> 16)
    tok_rows = d2 // LANES
    for j in range(tok_rows):
        xp_ref[pl.ds(j, tm, stride=tok_rows), :] = word[:, j * LANES:(j + 1) * LANES]


def _router(x2, nb, g, sh, sc, wr_t):
    m, d = x2.shape
    s = m // nb
    ne = wr_t.shape[0]
    tm = min(s, 512)
    nt = s // tm
    tok_rows = d // 2 // LANES
    mod_spec = pl.BlockSpec((None, 1, d), lambda b, i: (b, 0, 0))
    return pl.pallas_call(
        _router_kernel,
        grid=(nb, nt),
        in_specs=[pl.BlockSpec((tm, d), lambda b, i: (b * nt + i, 0)),
                  pl.BlockSpec((1, d), lambda b, i: (0, 0)), mod_spec, mod_spec,
                  pl.BlockSpec((ne, d), lambda b, i: (0, 0))],
        out_specs=[pl.BlockSpec((tm * tok_rows, LANES), lambda b, i: (b * nt + i, 0)),
                   pl.BlockSpec((None, ne, tm), lambda b, i: (b, 0, i))],
        out_shape=[jax.ShapeDtypeStruct((m * tok_rows, LANES), U32),
                   jax.ShapeDtypeStruct((nb, ne, s), F32)],
        compiler_params=_cparams("arbitrary", "arbitrary"),
    )(x2, g, sh, sc, wr_t)


def _prefix_incl(mask_bf):
    ne, s = mask_bf.shape
    r = lax.broadcasted_iota(I32, (LANES, LANES), 0)
    c = lax.broadcasted_iota(I32, (LANES, LANES), 1)
    tri = jnp.where(r <= c, 1.0, 0.0).astype(BF)
    carry = jnp.zeros((ne, 1), F32)
    out = []
    for ch in range(s // LANES):
        inc = _dot(mask_bf[:, ch * LANES:(ch + 1) * LANES], tri) + carry
        out.append(inc)
        carry = inc[:, LANES - 1:LANES]
    return jnp.concatenate(out, axis=1)


def _topk_kernel(aff_ref, pos_ref, cnt_ref, idx_ref, gslot_ref, pos_scr, *, cap):
    aff = aff_ref[...]
    ne, s = aff.shape
    v = lax.bitcast_convert_type(aff, I32)

    def bit_step(it, prefix):
        cand = prefix | jnp.left_shift(jnp.int32(1), 30 - it)
        cnt = jnp.sum(jnp.where(v >= cand, 1.0, 0.0), axis=1, keepdims=True)
        return jnp.where(cnt >= cap, cand, prefix)

    thr = lax.fori_loop(0, 31, bit_step, jnp.zeros((ne, 1), I32))
    gt = v > thr
    eq = v == thr
    need = cap - jnp.sum(jnp.where(gt, 1.0, 0.0), axis=1, keepdims=True)
    eq_bf = jnp.where(eq, 1.0, 0.0).astype(BF)
    eq_rank = _prefix_incl(eq_bf) - eq_bf.astype(F32)
    sel = gt | (eq & (eq_rank < need))
    sel_bf = jnp.where(sel, 1.0, 0.0).astype(BF)
    incl = _prefix_incl(sel_bf)
    pos = jnp.where(sel, incl - 1.0, -1.0).astype(I32)
    pos_ref[...] = pos
    t_i = lax.broadcasted_iota(I32, (s, LANES), 0)
    j_i = lax.broadcasted_iota(I32, (s, LANES), 1)
    before = jnp.where(t_i < j_i * LANES, 1.0, 0.0).astype(BF)
    cnt_ref[...] = _dot(sel_bf, before).astype(I32)
    pos_scr[...] = pos
    t_row = lax.broadcasted_iota(I32, (SUBLANES, s), 1)
    r_row = lax.broadcasted_iota(I32, (SUBLANES, s), 0)
    tval = jnp.where(r_row == 0, t_row // 64, jnp.where(r_row == 1, t_row % 64, 0)).astype(F32)
    slot = lax.broadcasted_iota(I32, (cap, 1), 0)

    def expert_step(e, carry):
        a = aff_ref[pl.ds(e, 1), :]
        a_hi = a.astype(BF).astype(F32)
        a_mid = (a - a_hi).astype(BF).astype(F32)
        a_lo = a - a_hi - a_mid
        vals = jnp.where(r_row == 2, a_hi, jnp.where(r_row == 3, a_mid, jnp.where(r_row == 4, a_lo, tval)))
        onehot = jnp.where(pos_scr[pl.ds(e, 1), :] == slot, 1.0, 0.0).astype(BF)
        res = _dot_nt(vals.astype(BF), onehot)
        idx_ref[pl.ds(e, 1), :] = (res[0:1] * 64.0 + res[1:2]).astype(I32)
        gslot_ref[pl.ds(e, 1), :] = res[2:3] + res[3:4] + res[4:5]
        return carry

    lax.fori_loop(0, ne, expert_step, 0)


def _topk(aff_t, cap):
    nb, ne, s = aff_t.shape
    blk = pl.BlockSpec((None, ne, s), lambda b: (b, 0, 0))
    return pl.pallas_call(
        functools.partial(_topk_kernel, cap=cap),
        grid=(nb,),
        in_specs=[blk],
        out_specs=[blk,
                   pl.BlockSpec((None, ne, LANES), lambda b: (b, 0, 0)),
                   pl.BlockSpec((None, ne, cap), lambda b: (b, 0, 0)),
                   pl.BlockSpec((None, ne, cap), lambda b: (b, 0, 0))],
        out_shape=[jax.ShapeDtypeStruct((nb, ne, s), I32),
                   jax.ShapeDtypeStruct((nb, ne, LANES), I32),
                   jax.ShapeDtypeStruct((nb, ne, cap), I32),
                   jax.ShapeDtypeStruct((nb, ne, cap), F32)],
        scratch_shapes=[pltpu.VMEM((ne, s), I32)],
        compiler_params=_cparams("arbitrary"),
    )(aff_t)


def _token_copy(src_ref, src_tok, dst_ref, dst_tok, tok_rows, sem):
    src = pl.multiple_of(src_tok * tok_rows, tok_rows)
    dst = pl.multiple_of(dst_tok * tok_rows, tok_rows)
    return pltpu.make_async_copy(src_ref.at[pl.ds(src, tok_rows), :], dst_ref.at[pl.ds(dst, tok_rows), :], sem)


def _expert_up_kernel(cur_ref, nxt_ref, xp_ref, wg_ref, wu_ref, o_ref, raw_ref, xb_ref, sem, *, ne, n, nj):
    e = pl.program_id(0)
    j = pl.program_id(1)
    slot = e % 2
    per = n // nj
    tok_rows = raw_ref.shape[1] // n

    def issue(idx_ref, dst, lo, cnt):
        def body(k, carry):
            c = lo + k
            _token_copy(xp_ref, idx_ref[0, c], raw_ref.at[dst], c, tok_rows, sem.at[dst]).start()
            return carry

        lax.fori_loop(0, cnt, body, 0, unroll=8)

    @pl.when((e == 0) & (j == 0))
    def _():
        issue(cur_ref, 0, 0, n)

    @pl.when(e + 1 < ne)
    def _():
        issue(nxt_ref, 1 - slot, j * per, per)

    @pl.when(j == 0)
    def _():
        def drain(c, carry):
            _token_copy(xp_ref, 0, raw_ref.at[slot], c, tok_rows, sem.at[slot]).wait()
            return carry

        lax.fori_loop(0, n, drain, 0, unroll=8)
        d2 = tok_rows * LANES
        for q in range(tok_rows):
            w = raw_ref[slot, pl.ds(q, n, stride=tok_rows), :]
            xb_ref[:, q * LANES:(q + 1) * LANES] = lax.bitcast_convert_type(
                w & jnp.uint32(0xFFFF0000), F32).astype(BF)
            xb_ref[:, d2 + q * LANES:d2 + (q + 1) * LANES] = lax.bitcast_convert_type(w << 16, F32).astype(BF)

    x = xb_ref[...]
    g = _dot(x, wg_ref[...].astype(BF))
    u = _dot(x, wu_ref[...].astype(BF))
    o_ref[...] = (g * jax.nn.sigmoid(g) * u).astype(BF)


def _expert_up(xp, rows, w_gate, w_up, layer):
    _, ne, d, f = w_gate.shape
    n = rows.shape[2]
    tf = min(f, 256)
    nj = f // tf
    tok_rows = d // 2 // LANES
    w_spec = pl.BlockSpec((None, None, d, tf), lambda e, j: (layer, e, 0, j))
    idx_spec = lambda f_: pl.BlockSpec((None, 1, n), f_, memory_space=pltpu.SMEM)
    return pl.pallas_call(
        functools.partial(_expert_up_kernel, ne=ne, n=n, nj=nj),
        grid=(ne, nj),
        in_specs=[idx_spec(lambda e, j: (e, 0, 0)),
                  idx_spec(lambda e, j: (jnp.minimum(e + 1, ne - 1), 0, 0)),
                  pl.BlockSpec(memory_space=pl.ANY), w_spec, w_spec],
        out_specs=pl.BlockSpec((None, n, tf), lambda e, j: (e, 0, j)),
        out_shape=jax.ShapeDtypeStruct((ne, n, f), BF),
        scratch_shapes=[pltpu.VMEM((2, n * tok_rows, LANES), U32), pltpu.VMEM((n, d), BF),
                        pltpu.SemaphoreType.DMA((2,))],
        compiler_params=_cparams("arbitrary", "arbitrary"),
    )(rows, rows, xp, w_gate, w_up)


def _expert_down_kernel(h_ref, w_ref, gs_ref, o_ref):
    o_ref[...] = _dot(h_ref[...], w_ref[...].astype(BF)) * gs_ref[...]


def _expert_down(hid, w_down, gslot, layer):
    ne, n, f = hid.shape
    d = w_down.shape[3]
    tn = min(d, 512)
    return pl.pallas_call(
        _expert_down_kernel,
        grid=(ne, d // tn),
        in_specs=[pl.BlockSpec((None, n, f), lambda e, j: (e, 0, 0)),
                  pl.BlockSpec((None, None, f, tn), lambda e, j: (layer, e, 0, j)),
                  pl.BlockSpec((None, n, 1), lambda e, j: (e, 0, 0))],
        out_specs=pl.BlockSpec((None, n, tn), lambda e, j: (e, 0, j)),
        out_shape=jax.ShapeDtypeStruct((ne, n, d), F32),
        compiler_params=_cparams("arbitrary", "arbitrary"),
    )(hid, w_down, gslot)


COMBINE_TOKENS = 128
COMBINE_CHUNK = 256
SLAB_ROWS = (128, 64, 32, 16, 8)
COMBINE_SLOTS_PER_EXPERT = COMBINE_TOKENS + SUBLANES


def _slab_copy(y_ref, src_row, buf_ref, dst_row, rows, sem):
    return pltpu.make_async_copy(y_ref.at[pl.ds(src_row, rows), :], buf_ref.at[pl.ds(dst_row, rows), :], sem)


def _combine_kernel(cnt_ref, y_ref, pos_ref, res_ref, g2_ref, o_ref, buf_ref, acc_ref, slot_ref, sem,
                    *, ne, n, cap, nbound):
    b = pl.program_id(0)
    t = pl.program_id(1)

    @pl.when((b == 0) & (t == 0))
    def _():
        buf_ref[...] = jnp.zeros_like(buf_ref)

    shift = []
    issued = []
    total = jnp.int32(0)
    for e in range(ne):
        base = (b * ne + e) * nbound + t
        c0 = cnt_ref[base]
        c1 = cnt_ref[base + 1]
        a0 = (c0 // SUBLANES) * SUBLANES
        run = jnp.where(c1 > c0, (c1 + SUBLANES - 1) // SUBLANES * SUBLANES - a0, 0)
        row0 = e * n + b * cap + a0
        done = jnp.int32(0)
        for rows in SLAB_ROWS:
            take = (run & rows) != 0
            src = pl.multiple_of(row0 + done, SUBLANES)
            dst = pl.multiple_of(total + done, SUBLANES)

            @pl.when(take)
            def _(src=src, dst=dst, rows=rows):
                _slab_copy(y_ref, src, buf_ref, dst, rows, sem).start()

            issued.append((take, rows))
            done = done + jnp.where(take, rows, 0)
        shift.append(total - a0)
        total = total + run

    for take, rows in issued:
        @pl.when(take)
        def _(rows=rows):
            _slab_copy(y_ref, 0, buf_ref, 0, rows, sem).wait()

    acc_ref[...] = jnp.zeros_like(acc_ref)
    pos = pos_ref[...]
    for e in range(ne):
        pe = pos[:, e:e + 1]
        where_e = jnp.where(pe >= 0, pe + shift[e], -1)
        slot_ref[e] = jnp.broadcast_to(where_e, (COMBINE_TOKENS, LANES))

    def chunk(ci, carry):
        start = pl.multiple_of(ci * COMBINE_CHUNK, COMBINE_CHUNK)
        parts = []
        for q in range(COMBINE_CHUNK // LANES):
            lane = lax.broadcasted_iota(I32, (COMBINE_TOKENS, LANES), 1) + (start + q * LANES)
            hot = jnp.zeros((COMBINE_TOKENS, LANES), F32)
            for e in range(ne):
                hot = jnp.where(slot_ref[e] == lane, 1.0, hot)
            parts.append(hot.astype(BF))
        onehot = jnp.concatenate(parts, axis=1)
        live = lax.broadcasted_iota(I32, (COMBINE_CHUNK, 1), 0) + start < total
        rows = jnp.where(live, buf_ref[pl.ds(start, COMBINE_CHUNK), :], 0.0)
        r_hi, r_lo = _split_bf16(rows)
        acc_ref[...] += _dot(onehot, r_hi) + _dot(onehot, r_lo)
        return carry

    lax.fori_loop(0, (total + COMBINE_CHUNK - 1) // COMBINE_CHUNK, chunk, 0)
    o_ref[...] = res_ref[...] + g2_ref[...] * acc_ref[...]


def _combine(cnt, y, pos_t, res, g2, nb, cap):
    m, d = res.shape
    s = m // nb
    ne = pos_t.shape[1]
    n = nb * cap
    nt = s // COMBINE_TOKENS
    nbound = nt + 1
    tok = lambda b, t, c: (b * nt + t, 0)
    grid_spec = pltpu.PrefetchScalarGridSpec(
        num_scalar_prefetch=1,
        grid=(nb, nt),
        in_specs=[pl.BlockSpec(memory_space=pl.ANY),
                  pl.BlockSpec((COMBINE_TOKENS, ne), tok),
                  pl.BlockSpec((COMBINE_TOKENS, d), tok),
                  pl.BlockSpec((None, 1, d), lambda b, t, c: (b, 0, 0))],
        out_specs=pl.BlockSpec((COMBINE_TOKENS, d), tok),
        scratch_shapes=[pltpu.VMEM((pl.cdiv(ne * COMBINE_SLOTS_PER_EXPERT, COMBINE_CHUNK) * COMBINE_CHUNK, d), F32),
                        pltpu.VMEM((COMBINE_TOKENS, d), F32),
                        pltpu.VMEM((ne, COMBINE_TOKENS, LANES), I32),
                        pltpu.SemaphoreType.DMA(())],
    )
    return pl.pallas_call(
        functools.partial(_combine_kernel, ne=ne, n=n, cap=cap, nbound=nbound),
        grid_spec=grid_spec,
        out_shape=jax.ShapeDtypeStruct((m, d), F32),
        compiler_params=_cparams("arbitrary", "arbitrary"),
    )(cnt[:, :, :nbound].reshape(-1), y, pos_t, res, g2)


def _moe(h2, nb, g, sh, sc, gate2, w_router, w_gate, w_up, w_down, layer):
    m, d = h2.shape
    s = m // nb
    ne = w_router.shape[1]
    cap = EC_FACTOR * s // ne
    xp, aff_t = _router(h2, nb, g, sh, sc, w_router.T)
    pos, cnt, idx, gslot = _topk(aff_t, cap)
    rows = idx + (jnp.arange(nb, dtype=I32) * s)[:, None, None]
    rows = jnp.swapaxes(rows, 0, 1).reshape(ne, 1, nb * cap)
    gslot = jnp.swapaxes(gslot, 0, 1).reshape(ne, nb * cap, 1)
    hid = _expert_up(xp, rows, w_gate, w_up, layer)
    y = _expert_down(hid, w_down, gslot, layer).reshape(ne * nb * cap, d)
    pos_t = jnp.swapaxes(pos, 1, 2).reshape(m, ne)
    return _combine(cnt, y, pos_t, h2, gate2, nb, cap)


def _rope_tables(s):
    rows = s // GRID_W
    r, col = jnp.meshgrid(jnp.arange(rows, dtype=F32), jnp.arange(GRID_W, dtype=F32), indexing="ij")
    r, col = r.reshape(-1), col.reshape(-1)
    n_pairs = HEAD_DIM // 4
    inv = ROPE_BASE ** (-jnp.arange(n_pairs, dtype=F32) / n_pairs)
    ang_r = r[:, None] * inv[None, :]
    ang_c = col[:, None] * inv[None, :]
    ang = jnp.concatenate([ang_r, ang_r, ang_c, ang_c], axis=-1)
    cos, sin = jnp.cos(ang), jnp.sin(ang)
    first = (jnp.arange(HEAD_DIM) % (HEAD_DIM // 2)) < HEAD_DIM // 4
    sin_a = jnp.where(first, -sin, 0.0)
    sin_b = jnp.where(first, 0.0, sin)
    two = lambda a: jnp.concatenate([a, a], axis=-1)
    return two(cos), two(sin_a), two(sin_b)


def kernel(x, c, ctx, c_ctx, ada_w, ada_b, norm1_g, norm2_g, attn_w_q, attn_w_kv, attn_w_o, attn_q_gain, attn_k_gain, attn_sink, conv_w_pw1, conv_b_pw1, conv_w_dw, conv_b_dw, conv_ln_g, conv_ln_b, conv_w_pw2, conv_b_pw2, moe_router, moe_w_gate, moe_w_up, moe_w_down):
    nb, s, d = x.shape
    lc = ctx.shape[1]
    depth = ada_w.shape[0]
    assert depth == 2, "layer 0 attention, layer 1 convolution; the context stream is never updated"
    m = nb * s

    cin = jnp.concatenate([c, c_ctx[None, :], jnp.zeros((8 - nb - 1, d), F32)], axis=0)
    mod = _ada(cin, ada_w, ada_b)

    def mods(layer, row0, nrow):
        return [mod[layer, row0:row0 + nrow, k * d:(k + 1) * d].reshape(nrow, 1, d) for k in range(6)]

    h = x.reshape(m, d)
    two = lambda a: jnp.concatenate([a, a], axis=-1).reshape(1, LANES)

    sh1, sc1, g1, sh2, sc2, g2 = mods(0, 0, nb)
    sh1c, sc1c = mods(0, nb, 1)[:2]
    n1 = norm1_g[0].reshape(1, d)
    wq = attn_w_q[0].astype(BF)
    wkv = attn_w_kv[0].astype(BF)
    wo = attn_w_o[0].astype(BF)
    q_gain = two(attn_q_gain[0]) * (HEAD_DIM ** -0.5 * LOG2E)
    k_gain = two(attn_k_gain[0])
    q, k, v = _qkv_proj(h, nb, n1, sh1, sc1, wq, wkv, q_gain, k_gain, _rope_tables(s))
    kx, vx = _qkv_proj(ctx.reshape(nb * lc, d), nb, n1, sh1c, sc1c, None, wkv, None, k_gain, None)
    att = _attention(q, k, v, kx, vx, attn_sink[0])
    h = _mm_res(att, wo, h, g1, nb)
    h = _moe(h, nb, norm2_g[0].reshape(1, d), sh2, sc2, g2,
             moe_router[0], moe_w_gate, moe_w_up, moe_w_down, 0)

    sh1, sc1, g1, sh2, sc2, g2 = mods(1, 0, nb)
    u = _pw1_glu(h, nb, norm1_g[1].reshape(1, d), sh1, sc1,
                 conv_w_pw1[0].astype(BF), conv_b_pw1[0].reshape(1, 2 * d))
    h = _conv_pw2(u, nb, conv_w_dw[0], conv_b_dw[0].reshape(1, d), conv_ln_g[0].reshape(1, d),
                  conv_ln_b[0].reshape(1, d), conv_w_pw2[0].astype(BF), conv_b_pw2[0].reshape(1, d), h, g1)
    h = _moe(h, nb, norm2_g[1].reshape(1, d), sh2, sc2, g2,
             moe_router[1], moe_w_gate, moe_w_up, moe_w_down, 1)
    return h.reshape(nb, s, d)
```

```python
import functools

import jax
import jax.numpy as jnp
from jax import lax
from jax.experimental import pallas as pl
from jax.experimental.pallas import tpu as pltpu

F32 = jnp.float32
BF = jnp.bfloat16
I32 = jnp.int32
U32 = jnp.uint32

HEAD_DIM = 64
GROUP = 4
GRID_W = 64
ROPE_BASE = 10000.0
ATT_BLOCK = 128
CONV_WIDTH = 31
CONV_HALO = 16
CONV_ROWS = 64
SUBLANES = 8
N_EXPERTS = 16
EC_FACTOR = 2
EPS = 1e-6
NEG_INF = -1e30
LOG2E = 1.4426950408889634
LANES = 128
V7X_VMEM_LIMIT = 56 * 1024 * 1024


def _cparams(*sem):
    return pltpu.CompilerParams(dimension_semantics=sem, vmem_limit_bytes=V7X_VMEM_LIMIT)


def _dot(a, b):
    return jnp.dot(a, b, preferred_element_type=F32)


def _dot_nt(a, b):
    return lax.dot_general(a, b, (((1,), (1,)), ((), ())), preferred_element_type=F32)


def _split_bf16(x):
    hi = x.astype(BF)
    lo = (x - hi.astype(F32)).astype(BF)
    return hi, lo


def _norm_mod(x, g, sh, sc):
    ms = jnp.mean(x * x, axis=-1, keepdims=True)
    return (x * lax.rsqrt(ms + EPS) * g) * (1.0 + sc) + sh


def _ada_kernel(c_ref, w_ref, b_ref, o_ref):
    c = c_ref[...]
    s = c * jax.nn.sigmoid(c)
    o_ref[...] = _dot(s.astype(BF), w_ref[...].astype(BF)) + b_ref[...]


def _ada(cin, ada_w, ada_b):
    depth, d, n = ada_w.shape
    rows = cin.shape[0]
    tn = min(n, 1024)
    return pl.pallas_call(
        _ada_kernel,
        grid=(depth, n // tn),
        in_specs=[pl.BlockSpec((rows, d), lambda l, j: (0, 0)),
                  pl.BlockSpec((None, d, tn), lambda l, j: (l, 0, j)),
                  pl.BlockSpec((None, 1, tn), lambda l, j: (l, 0, j))],
        out_specs=pl.BlockSpec((None, rows, tn), lambda l, j: (l, 0, j)),
        out_shape=jax.ShapeDtypeStruct((depth, rows, n), F32),
        compiler_params=_cparams("arbitrary", "arbitrary"),
    )(cin, ada_w, ada_b.reshape(depth, 1, n))


def _head_norm(y, gain):
    w = y.shape[1]
    r = lax.broadcasted_iota(I32, (w, w), 0) // HEAD_DIM
    c = lax.broadcasted_iota(I32, (w, w), 1) // HEAD_DIM
    bd = jnp.where(r == c, 1.0 / HEAD_DIM, 0.0).astype(BF)
    hi, lo = _split_bf16(y * y)
    ms = _dot(hi, bd) + _dot(lo, bd)
    return y * lax.rsqrt(ms + EPS) * jnp.concatenate([gain] * (w // LANES), axis=1)


def _rope(y, cos, sin_a, sin_b):
    return (y * cos + pltpu.roll(y, LANES - HEAD_DIM // 4, 1) * sin_a
            + pltpu.roll(y, HEAD_DIM // 4, 1) * sin_b)


def _qkv_kernel(*refs, nh, nkv, rope, with_q):
    if with_q:
        (x_ref, g_ref, sh_ref, sc_ref, wq_ref, wkv_ref, qg_ref, kg_ref,
         cos_ref, sa_ref, sb_ref, q_ref, k_ref, v_ref) = refs
    else:
        (x_ref, g_ref, sh_ref, sc_ref, wkv_ref, kg_ref, k_ref, v_ref) = refs
    hb = _norm_mod(x_ref[...], g_ref[...], sh_ref[...], sc_ref[...]).astype(BF)
    if rope:
        cos, sin_a, sin_b = cos_ref[...], sa_ref[...], sb_ref[...]
    kvd = nkv * HEAD_DIM
    kv = _dot(hb, wkv_ref[...])
    def finish(y_wide, gain, out_ref, head0):
        yn = _head_norm(y_wide, gain)
        for part in range(y_wide.shape[1] // LANES):
            y = yn[:, part * LANES:(part + 1) * LANES]
            if rope:
                y = _rope(y, cos, sin_a, sin_b)
            yb = y.astype(BF)
            out_ref[head0 + 2 * part] = yb[:, :HEAD_DIM]
            out_ref[head0 + 2 * part + 1] = yb[:, HEAD_DIM:]

    cw = min(2 * LANES, kvd)
    per = cw // HEAD_DIM
    for c in range(kvd // cw):
        finish(kv[:, c * cw:(c + 1) * cw], kg_ref[...], k_ref, per * c)
    for h in range(nkv):
        v_ref[h] = kv[:, kvd + h * HEAD_DIM:kvd + (h + 1) * HEAD_DIM].astype(BF)
    if with_q:
        for c in range(nh * HEAD_DIM // cw):
            finish(_dot(hb, wq_ref[:, c * cw:(c + 1) * cw]), qg_ref[...], q_ref, per * c)


def _qkv_proj(x2, nb, g, sh, sc, wq, wkv, qg, kg, tables):
    m, d = x2.shape
    s = m // nb
    nkv = wkv.shape[1] // (2 * HEAD_DIM)
    with_q = wq is not None
    tm = min(s, 256)
    nt = s // tm
    shared = sh.shape[0] == 1
    mod_map = (lambda b, i: (0, 0, 0)) if shared else (lambda b, i: (b, 0, 0))
    row_spec = pl.BlockSpec((tm, d), lambda b, i: (b * nt + i, 0))
    vec_d = pl.BlockSpec((1, d), lambda b, i: (0, 0))
    mod_spec = pl.BlockSpec((None, 1, d), mod_map)
    vec_l = pl.BlockSpec((1, LANES), lambda b, i: (0, 0))
    tab_spec = pl.BlockSpec((tm, LANES), lambda b, i: (i, 0))
    kv_spec = pl.BlockSpec((None, nkv, tm, HEAD_DIM), lambda b, i: (b, 0, i, 0))
    kv_shape = jax.ShapeDtypeStruct((nb, nkv, s, HEAD_DIM), BF)
    if with_q:
        nh = wq.shape[1] // HEAD_DIM
        ins = [x2, g, sh, sc, wq, wkv, qg, kg, *tables]
        in_specs = [row_spec, vec_d, mod_spec, mod_spec,
                    pl.BlockSpec(wq.shape, lambda b, i: (0, 0)),
                    pl.BlockSpec(wkv.shape, lambda b, i: (0, 0)),
                    vec_l, vec_l, tab_spec, tab_spec, tab_spec]
        out_specs = [pl.BlockSpec((None, nh, tm, HEAD_DIM), lambda b, i: (b, 0, i, 0)), kv_spec, kv_spec]
        out_shape = [jax.ShapeDtypeStruct((nb, nh, s, HEAD_DIM), BF), kv_shape, kv_shape]
    else:
        nh = 0
        ins = [x2, g, sh, sc, wkv, kg]
        in_specs = [row_spec, vec_d, mod_spec, mod_spec,
                    pl.BlockSpec(wkv.shape, lambda b, i: (0, 0)), vec_l]
        out_specs = [kv_spec, kv_spec]
        out_shape = [kv_shape, kv_shape]
    return pl.pallas_call(
        functools.partial(_qkv_kernel, nh=nh, nkv=nkv, rope=with_q, with_q=with_q),
        grid=(nb, nt), in_specs=in_specs, out_specs=out_specs, out_shape=out_shape,
        compiler_params=_cparams("arbitrary", "arbitrary"),
    )(*ins)


def _attn_kernel(sink_ref, q_ref, kp_ref, kc_ref, kn_ref, vp_ref, vc_ref, vn_ref,
                 kx_ref, vx_ref, o_ref, *, nkv, nblk):
    i = pl.program_id(1)
    rows = GROUP * ATT_BLOCK
    qi = lax.broadcasted_iota(I32, (rows, ATT_BLOCK), 0) % ATT_BLOCK
    kj = lax.broadcasted_iota(I32, (rows, ATT_BLOCK), 1)
    prev_ok = (kj >= qi) & (i > 0)
    next_ok = (kj <= qi) & (i < nblk - 1)
    for h in range(nkv):
        q = q_ref[GROUP * h:GROUP * (h + 1)].reshape(rows, HEAD_DIM)
        s = jnp.concatenate([jnp.where(prev_ok, _dot_nt(q, kp_ref[h]), NEG_INF),
                             _dot_nt(q, kc_ref[h]),
                             jnp.where(next_ok, _dot_nt(q, kn_ref[h]), NEG_INF),
                             _dot_nt(q, kx_ref[h])], axis=1)
        sink = jnp.concatenate(
            [jnp.full((ATT_BLOCK, 1), sink_ref[GROUP * h + g] * LOG2E, F32) for g in range(GROUP)], axis=0)
        m = jnp.maximum(jnp.max(s, axis=-1, keepdims=True), sink)
        p = jnp.exp2(s - m)
        den = jnp.sum(p, axis=-1, keepdims=True) + jnp.exp2(sink - m)
        pb = p.astype(BF)
        w = ATT_BLOCK
        o = (_dot(pb[:, :w], vp_ref[h]) + _dot(pb[:, w:2 * w], vc_ref[h])
             + _dot(pb[:, 2 * w:3 * w], vn_ref[h]) + _dot(pb[:, 3 * w:], vx_ref[h]))
        o = (o / den).astype(BF)
        for g in range(GROUP):
            hh = GROUP * h + g
            o_ref[:, hh * HEAD_DIM:(hh + 1) * HEAD_DIM] = o[g * ATT_BLOCK:(g + 1) * ATT_BLOCK]


def _attention(q, k, v, kx, vx, sink):
    nb, nh, s, _ = q.shape
    nkv = k.shape[1]
    lc = kx.shape[2]
    nblk = s // ATT_BLOCK
    blk = lambda f: pl.BlockSpec((None, nkv, ATT_BLOCK, HEAD_DIM), f)
    prev = lambda b, i: (b, 0, jnp.maximum(i - 1, 0), 0)
    cur = lambda b, i: (b, 0, i, 0)
    nxt = lambda b, i: (b, 0, jnp.minimum(i + 1, nblk - 1), 0)
    ctx_spec = pl.BlockSpec((None, nkv, lc, HEAD_DIM), lambda b, i: (b, 0, 0, 0))
    return pl.pallas_call(
        functools.partial(_attn_kernel, nkv=nkv, nblk=nblk),
        grid=(nb, nblk),
        in_specs=[pl.BlockSpec(memory_space=pltpu.SMEM),
                  pl.BlockSpec((None, nh, ATT_BLOCK, HEAD_DIM), cur),
                  blk(prev), blk(cur), blk(nxt), blk(prev), blk(cur), blk(nxt), ctx_spec, ctx_spec],
        out_specs=pl.BlockSpec((ATT_BLOCK, nh * HEAD_DIM), lambda b, i: (b * nblk + i, 0)),
        out_shape=jax.ShapeDtypeStruct((nb * s, nh * HEAD_DIM), BF),
        compiler_params=_cparams("arbitrary", "arbitrary"),
    )(sink, q, k, k, k, v, v, v, kx, vx)


def _mm_res_kernel(a_ref, w_ref, res_ref, gate_ref, o_ref):
    o_ref[...] = res_ref[...] + gate_ref[...] * _dot(a_ref[...], w_ref[...])


def _mm_res(a, w, res, gate, nb):
    m, kd = a.shape
    n = w.shape[1]
    tm = min(m // nb, 512)
    per_b = m // nb // tm
    return pl.pallas_call(
        _mm_res_kernel,
        grid=(m // tm,),
        in_specs=[pl.BlockSpec((tm, kd), lambda i: (i, 0)),
                  pl.BlockSpec((kd, n), lambda i: (0, 0)),
                  pl.BlockSpec((tm, n), lambda i: (i, 0)),
                  pl.BlockSpec((None, 1, n), lambda i: (i // per_b, 0, 0))],
        out_specs=pl.BlockSpec((tm, n), lambda i: (i, 0)),
        out_shape=jax.ShapeDtypeStruct((m, n), F32),
        compiler_params=_cparams("arbitrary"),
    )(a, w, res, gate)


def _pw1_kernel(x_ref, g_ref, sh_ref, sc_ref, wa_ref, wg_ref, ba_ref, bg_ref, o_ref):
    hb = _norm_mod(x_ref[...], g_ref[...], sh_ref[...], sc_ref[...]).astype(BF)
    a = _dot(hb, wa_ref[...]) + ba_ref[...]
    gt = _dot(hb, wg_ref[...]) + bg_ref[...]
    o_ref[...] = a * jax.nn.sigmoid(gt)


def _pw1_glu(x2, nb, g, sh, sc, w, bias):
    m, d = x2.shape
    tm = min(m // nb, 512)
    per_b = m // nb // tm
    tn = min(d, 1024)
    nj = d // tn
    mod_spec = pl.BlockSpec((None, 1, d), lambda j, i: (i // per_b, 0, 0))
    return pl.pallas_call(
        _pw1_kernel,
        grid=(nj, m // tm),
        in_specs=[pl.BlockSpec((tm, d), lambda j, i: (i, 0)),
                  pl.BlockSpec((1, d), lambda j, i: (0, 0)), mod_spec, mod_spec,
                  pl.BlockSpec((d, tn), lambda j, i: (0, j)),
                  pl.BlockSpec((d, tn), lambda j, i: (0, nj + j)),
                  pl.BlockSpec((1, tn), lambda j, i: (0, j)),
                  pl.BlockSpec((1, tn), lambda j, i: (0, nj + j))],
        out_specs=pl.BlockSpec((tm, tn), lambda j, i: (i, j)),
        out_shape=jax.ShapeDtypeStruct((m, d), F32),
        compiler_params=_cparams("arbitrary", "arbitrary"),
    )(x2, g, sh, sc, w, w, bias, bias)


def _conv_kernel(up_ref, uc_ref, un_ref, wdw_ref, bdw_ref, lg_ref, lb_ref, w2_ref, b2_ref,
                 res_ref, gate_ref, o_ref, ext_ref, cv_ref, *, nt):
    i = pl.program_id(1)
    tm, d = uc_ref.shape
    zero = jnp.zeros((CONV_HALO, d), F32)
    ext_ref[0:CONV_HALO, :] = jnp.where(i > 0, up_ref[...], zero)
    ext_ref[CONV_HALO:CONV_HALO + tm, :] = uc_ref[...]
    ext_ref[CONV_HALO + tm:, :] = jnp.where(i < nt - 1, un_ref[...], zero)
    off = CONV_HALO - CONV_WIDTH // 2
    rb = min(tm, CONV_ROWS)
    span = rb + 2 * CONV_HALO
    for c in range(d // LANES):
        cols = slice(c * LANES, (c + 1) * LANES)
        wcol = wdw_ref[:, cols]
        for r0 in range(0, tm, rb):
            xin = ext_ref[r0:r0 + span, cols]
            shifted = {0: xin}
            acc = jnp.broadcast_to(bdw_ref[:, cols], (rb, LANES))
            for k in range(CONV_WIDTH):
                r, a = (off + k) % SUBLANES, (off + k) // SUBLANES
                if r not in shifted:
                    shifted[r] = pltpu.roll(xin, span - r, 0)
                acc = acc + wcol[k:k + 1] * shifted[r][SUBLANES * a:SUBLANES * a + rb]
            cv_ref[r0:r0 + rb, cols] = acc
    u = cv_ref[...]
    mu = jnp.mean(u, axis=-1, keepdims=True)
    var = jnp.mean(jnp.square(u - mu), axis=-1, keepdims=True)
    y = (u - mu) * lax.rsqrt(var + EPS) * lg_ref[...] + lb_ref[...]
    y = y * jax.nn.sigmoid(y)
    out = _dot(y.astype(BF), w2_ref[...]) + b2_ref[...]
    o_ref[...] = res_ref[...] + gate_ref[...] * out


def _conv_pw2(u, nb, wdw, bdw, lg, lb, w2, b2, res, gate):
    m, d = u.shape
    s = m // nb
    tm = min(s, 256)
    nt = s // tm
    hb = tm // CONV_HALO
    nhb = s // CONV_HALO
    vec = pl.BlockSpec((1, d), lambda b, i: (0, 0))
    tile = pl.BlockSpec((tm, d), lambda b, i: (b * nt + i, 0))
    return pl.pallas_call(
        functools.partial(_conv_kernel, nt=nt),
        grid=(nb, nt),
        in_specs=[pl.BlockSpec((CONV_HALO, d), lambda b, i: (b * nhb + jnp.maximum(i * hb - 1, 0), 0)),
                  tile,
                  pl.BlockSpec((CONV_HALO, d), lambda b, i: (b * nhb + jnp.minimum((i + 1) * hb, nhb - 1), 0)),
                  pl.BlockSpec((CONV_WIDTH, d), lambda b, i: (0, 0)),
                  vec, vec, vec,
                  pl.BlockSpec((d, d), lambda b, i: (0, 0)),
                  vec, tile,
                  pl.BlockSpec((None, 1, d), lambda b, i: (b, 0, 0))],
        out_specs=tile,
        out_shape=jax.ShapeDtypeStruct((m, d), F32),
        scratch_shapes=[pltpu.VMEM((tm + 2 * CONV_HALO, d), F32), pltpu.VMEM((tm, d), F32)],
        compiler_params=_cparams("arbitrary", "arbitrary"),
    )(u, u, u, wdw, bdw, lg, lb, w2, b2, res, gate)


def _router_kernel(x_ref, g_ref, sh_ref, sc_ref, wr_ref, xp_ref, aff_ref):
    hn = _norm_mod(x_ref[...], g_ref[...], sh_ref[...], sc_ref[...])
    h_hi, h_lo = _split_bf16(hn)
    w_hi, w_lo = _split_bf16(wr_ref[...])
    logits = _dot_nt(w_hi, h_hi) + _dot_nt(w_hi, h_lo) + _dot_nt(w_lo, h_hi)
    mx = jnp.max(logits, axis=0, keepdims=True)
    ex = jnp.exp(logits - mx)
    aff_ref[...] = ex / jnp.sum(ex, axis=0, keepdims=True)
    bits = lax.bitcast_convert_type(h_hi.astype(F32), U32)
    tm = bits.shape[0]
    d2 = bits.shape[1] // 2
    word = bits[:, :d2] | (bits[:, d2:] >> 16)
    tok_rows = d2 // LANES
    for j in range(tok_rows):
        xp_ref[pl.ds(j, tm, stride=tok_rows), :] = word[:, j * LANES:(j + 1) * LANES]


def _router(x2, nb, g, sh, sc, wr_t):
    m, d = x2.shape
    s = m // nb
    ne = wr_t.shape[0]
    tm = min(s, 512)
    nt = s // tm
    tok_rows = d // 2 // LANES
    mod_spec = pl.BlockSpec((None, 1, d), lambda b, i: (b, 0, 0))
    return pl.pallas_call(
        _router_kernel,
        grid=(nb, nt),
        in_specs=[pl.BlockSpec((tm, d), lambda b, i: (b * nt + i, 0)),
                  pl.BlockSpec((1, d), lambda b, i: (0, 0)), mod_spec, mod_spec,
                  pl.BlockSpec((ne, d), lambda b, i: (0, 0))],
        out_specs=[pl.BlockSpec((tm * tok_rows, LANES), lambda b, i: (b * nt + i, 0)),
                   pl.BlockSpec((None, ne, tm), lambda b, i: (b, 0, i))],
        out_shape=[jax.ShapeDtypeStruct((m * tok_rows, LANES), U32),
                   jax.ShapeDtypeStruct((nb, ne, s), F32)],
        compiler_params=_cparams("arbitrary", "arbitrary"),
    )(x2, g, sh, sc, wr_t)


def _prefix_incl(mask_bf):
    ne, s = mask_bf.shape
    r = lax.broadcasted_iota(I32, (LANES, LANES), 0)
    c = lax.broadcasted_iota(I32, (LANES, LANES), 1)
    tri = jnp.where(r <= c, 1.0, 0.0).astype(BF)
    carry = jnp.zeros((ne, 1), F32)
    out = []
    for ch in range(s // LANES):
        inc = _dot(mask_bf[:, ch * LANES:(ch + 1) * LANES], tri) + carry
        out.append(inc)
        carry = inc[:, LANES - 1:LANES]
    return jnp.concatenate(out, axis=1)


def _topk_kernel(aff_ref, pos_ref, cnt_ref, idx_ref, gslot_ref, pos_scr, *, cap):
    aff = aff_ref[...]
    ne, s = aff.shape
    v = lax.bitcast_convert_type(aff, I32)

    def bit_step(it, prefix):
        cand = prefix | jnp.left_shift(jnp.int32(1), 30 - it)
        cnt = jnp.sum(jnp.where(v >= cand, 1.0, 0.0), axis=1, keepdims=True)
        return jnp.where(cnt >= cap, cand, prefix)

    thr = lax.fori_loop(0, 31, bit_step, jnp.zeros((ne, 1), I32))
    gt = v > thr
    eq = v == thr
    need = cap - jnp.sum(jnp.where(gt, 1.0, 0.0), axis=1, keepdims=True)
    eq_bf = jnp.where(eq, 1.0, 0.0).astype(BF)
    eq_rank = _prefix_incl(eq_bf) - eq_bf.astype(F32)
    sel = gt | (eq & (eq_rank < need))
    sel_bf = jnp.where(sel, 1.0, 0.0).astype(BF)
    incl = _prefix_incl(sel_bf)
    pos = jnp.where(sel, incl - 1.0, -1.0).astype(I32)
    pos_ref[...] = pos
    t_i = lax.broadcasted_iota(I32, (s, LANES), 0)
    j_i = lax.broadcasted_iota(I32, (s, LANES), 1)
    before = jnp.where(t_i < j_i * LANES, 1.0, 0.0).astype(BF)
    cnt_ref[...] = _dot(sel_bf, before).astype(I32)
    pos_scr[...] = pos
    t_row = lax.broadcasted_iota(I32, (SUBLANES, s), 1)
    r_row = lax.broadcasted_iota(I32, (SUBLANES, s), 0)
    tval = jnp.where(r_row == 0, t_row // 64, jnp.where(r_row == 1, t_row % 64, 0)).astype(F32)
    slot = lax.broadcasted_iota(I32, (cap, 1), 0)

    def expert_step(e, carry):
        a = aff_ref[pl.ds(e, 1), :]
        a_hi = a.astype(BF).astype(F32)
        a_mid = (a - a_hi).astype(BF).astype(F32)
        a_lo = a - a_hi - a_mid
        vals = jnp.where(r_row == 2, a_hi, jnp.where(r_row == 3, a_mid, jnp.where(r_row == 4, a_lo, tval)))
        onehot = jnp.where(pos_scr[pl.ds(e, 1), :] == slot, 1.0, 0.0).astype(BF)
        res = _dot_nt(vals.astype(BF), onehot)
        idx_ref[pl.ds(e, 1), :] = (res[0:1] * 64.0 + res[1:2]).astype(I32)
        gslot_ref[pl.ds(e, 1), :] = res[2:3] + res[3:4] + res[4:5]
        return carry

    lax.fori_loop(0, ne, expert_step, 0)


def _topk(aff_t, cap):
    nb, ne, s = aff_t.shape
    blk = pl.BlockSpec((None, ne, s), lambda b: (b, 0, 0))
    return pl.pallas_call(
        functools.partial(_topk_kernel, cap=cap),
        grid=(nb,),
        in_specs=[blk],
        out_specs=[blk,
                   pl.BlockSpec((None, ne, LANES), lambda b: (b, 0, 0)),
                   pl.BlockSpec((None, ne, cap), lambda b: (b, 0, 0)),
                   pl.BlockSpec((None, ne, cap), lambda b: (b, 0, 0))],
        out_shape=[jax.ShapeDtypeStruct((nb, ne, s), I32),
                   jax.ShapeDtypeStruct((nb, ne, LANES), I32),
                   jax.ShapeDtypeStruct((nb, ne, cap), I32),
                   jax.ShapeDtypeStruct((nb, ne, cap), F32)],
        scratch_shapes=[pltpu.VMEM((ne, s), I32)],
        compiler_params=_cparams("arbitrary"),
    )(aff_t)


GATHER_UNROLL = 8


def _token_copy(src_ref, src_tok, dst_ref, dst_tok, tok_rows, sem):
    src = pl.multiple_of(src_tok * tok_rows, tok_rows)
    dst = pl.multiple_of(dst_tok * tok_rows, tok_rows)
    return pltpu.make_async_copy(src_ref.at[pl.ds(src, tok_rows), :], dst_ref.at[pl.ds(dst, tok_rows), :], sem)


def _expert_up_kernel(cur_ref, nxt_ref, xp_ref, wg_ref, wu_ref, o_ref, raw_ref, xb_ref, sem, *, ne, n, nj):
    e = pl.program_id(0)
    j = pl.program_id(1)
    slot = e % 2
    per = n // nj
    tok_rows = raw_ref.shape[1] // n

    def issue(idx_ref, dst, lo, cnt):
        def body(k, carry):
            for i in range(GATHER_UNROLL):
                c = lo + k * GATHER_UNROLL + i
                _token_copy(xp_ref, idx_ref[0, c], raw_ref.at[dst], c, tok_rows, sem.at[dst]).start(priority=i % 2)
            return carry

        lax.fori_loop(0, cnt // GATHER_UNROLL, body, 0)

    @pl.when((e == 0) & (j == 0))
    def _():
        issue(cur_ref, 0, 0, n)

    @pl.when(e + 1 < ne)
    def _():
        issue(nxt_ref, 1 - slot, j * per, per)

    @pl.when(j == 0)
    def _():
        def drain(c, carry):
            _token_copy(xp_ref, 0, raw_ref.at[slot], c, tok_rows, sem.at[slot]).wait()
            return carry

        lax.fori_loop(0, n, drain, 0, unroll=8)
        d2 = tok_rows * LANES
        for q in range(tok_rows):
            w = raw_ref[slot, pl.ds(q, n, stride=tok_rows), :]
            xb_ref[:, q * LANES:(q + 1) * LANES] = lax.bitcast_convert_type(
                w & jnp.uint32(0xFFFF0000), F32).astype(BF)
            xb_ref[:, d2 + q * LANES:d2 + (q + 1) * LANES] = lax.bitcast_convert_type(w << 16, F32).astype(BF)

    x = xb_ref[...]
    g = _dot(x, wg_ref[...].astype(BF))
    u = _dot(x, wu_ref[...].astype(BF))
    o_ref[...] = (g * jax.nn.sigmoid(g) * u).astype(BF)


def _expert_up(xp, rows, w_gate, w_up, layer):
    _, ne, d, f = w_gate.shape
    n = rows.shape[2]
    tf = min(f, 256)
    nj = f // tf
    tok_rows = d // 2 // LANES
    w_spec = pl.BlockSpec((None, None, d, tf), lambda e, j: (layer, e, 0, j))
    idx_spec = lambda f_: pl.BlockSpec((None, 1, n), f_, memory_space=pltpu.SMEM)
    return pl.pallas_call(
        functools.partial(_expert_up_kernel, ne=ne, n=n, nj=nj),
        grid=(ne, nj),
        in_specs=[idx_spec(lambda e, j: (e, 0, 0)),
                  idx_spec(lambda e, j: (jnp.minimum(e + 1, ne - 1), 0, 0)),
                  pl.BlockSpec(memory_space=pl.ANY), w_spec, w_spec],
        out_specs=pl.BlockSpec((None, n, tf), lambda e, j: (e, 0, j)),
        out_shape=jax.ShapeDtypeStruct((ne, n, f), BF),
        scratch_shapes=[pltpu.VMEM((2, n * tok_rows, LANES), U32), pltpu.VMEM((n, d), BF),
                        pltpu.SemaphoreType.DMA((2,))],
        compiler_params=_cparams("arbitrary", "arbitrary"),
    )(rows, rows, xp, w_gate, w_up)


def _expert_down_kernel(h_ref, w_ref, gs_ref, o_ref):
    o_ref[...] = _dot(h_ref[...], w_ref[...].astype(BF)) * gs_ref[...]


def _expert_down(hid, w_down, gslot, layer):
    ne, n, f = hid.shape
    d = w_down.shape[3]
    tn = min(d, 512)
    return pl.pallas_call(
        _expert_down_kernel,
        grid=(ne, d // tn),
        in_specs=[pl.BlockSpec((None, n, f), lambda e, j: (e, 0, 0)),
                  pl.BlockSpec((None, None, f, tn), lambda e, j: (layer, e, 0, j)),
                  pl.BlockSpec((None, n, 1), lambda e, j: (e, 0, 0))],
        out_specs=pl.BlockSpec((None, n, tn), lambda e, j: (e, 0, j)),
        out_shape=jax.ShapeDtypeStruct((ne, n, d), F32),
        compiler_params=_cparams("arbitrary", "arbitrary"),
    )(hid, w_down, gslot)


COMBINE_TOKENS = 128
COMBINE_CHUNK = 256
SLAB_ROWS = (128, 64, 32, 16, 8)
COMBINE_SLOTS_PER_EXPERT = COMBINE_TOKENS + SUBLANES


def _slab_copy(y_ref, src_row, buf_ref, dst_row, rows, sem):
    return pltpu.make_async_copy(y_ref.at[pl.ds(src_row, rows), :], buf_ref.at[pl.ds(dst_row, rows), :], sem)


def _combine_kernel(cnt_ref, y_ref, pos_ref, res_ref, g2_ref, o_ref, buf_ref, acc_ref, slot_ref, sem,
                    *, ne, n, cap, nbound):
    b = pl.program_id(0)
    t = pl.program_id(1)
    nt = nbound - 1
    step = b * nt + t
    last = pl.num_programs(0) * nt - 1

    @pl.when(step == 0)
    def _():
        buf_ref[...] = jnp.zeros_like(buf_ref)

    def plan(bq, tq):
        slabs, shift = [], []
        total = jnp.int32(0)
        for e in range(ne):
            base = (bq * ne + e) * nbound + tq
            c0 = cnt_ref[base]
            c1 = cnt_ref[base + 1]
            a0 = (c0 // SUBLANES) * SUBLANES
            run = jnp.where(c1 > c0, (c1 + SUBLANES - 1) // SUBLANES * SUBLANES - a0, 0)
            row0 = e * n + bq * cap + a0
            done = jnp.int32(0)
            for rows in SLAB_ROWS:
                take = (run & rows) != 0
                slabs.append((take, rows, pl.multiple_of(row0 + done, SUBLANES),
                              pl.multiple_of(total + done, SUBLANES)))
                done = done + jnp.where(take, rows, 0)
            shift.append(total - a0)
            total = total + run
        return slabs, shift, total

    def start_all(slabs, half):
        for take, rows, src, dst in slabs:
            @pl.when(take)
            def _(rows=rows, src=src, dst=dst):
                _slab_copy(y_ref, src, buf_ref.at[half], dst, rows, sem.at[half]).start()

    half = step % 2
    slabs, shift, total = plan(b, t)

    @pl.when(step == 0)
    def _():
        start_all(slabs, half)

    nxt = jnp.minimum(step + 1, last)
    nxt_slabs, _, _ = plan(nxt // nt, nxt % nt)

    @pl.when(step < last)
    def _():
        start_all(nxt_slabs, 1 - half)

    for take, rows, _, _ in slabs:
        @pl.when(take)
        def _(rows=rows):
            _slab_copy(y_ref, 0, buf_ref.at[half], 0, rows, sem.at[half]).wait()

    acc_ref[...] = jnp.zeros_like(acc_ref)
    pos = pos_ref[...]
    for e in range(ne):
        pe = pos[:, e:e + 1]
        where_e = jnp.where(pe >= 0, pe + shift[e], -1)
        slot_ref[e] = jnp.broadcast_to(where_e, (COMBINE_TOKENS, LANES))

    def chunk(ci, carry):
        start = pl.multiple_of(ci * COMBINE_CHUNK, COMBINE_CHUNK)
        parts = []
        for q in range(COMBINE_CHUNK // LANES):
            lane = lax.broadcasted_iota(I32, (COMBINE_TOKENS, LANES), 1) + (start + q * LANES)
            hot = jnp.zeros((COMBINE_TOKENS, LANES), F32)
            for e in range(ne):
                hot = jnp.where(slot_ref[e] == lane, 1.0, hot)
            parts.append(hot.astype(BF))
        onehot = jnp.concatenate(parts, axis=1)
        live = lax.broadcasted_iota(I32, (COMBINE_CHUNK, 1), 0) + start < total
        rows = jnp.where(live, buf_ref[half, pl.ds(start, COMBINE_CHUNK), :], 0.0)
        r_hi, r_lo = _split_bf16(rows)
        acc_ref[...] += _dot(onehot, r_hi) + _dot(onehot, r_lo)
        return carry

    lax.fori_loop(0, (total + COMBINE_CHUNK - 1) // COMBINE_CHUNK, chunk, 0)
    o_ref[...] = res_ref[...] + g2_ref[...] * acc_ref[...]


def _combine(cnt, y, pos_t, res, g2, nb, cap):
    m, d = res.shape
    s = m // nb
    ne = pos_t.shape[1]
    n = nb * cap
    nt = s // COMBINE_TOKENS
    nbound = nt + 1
    tok = lambda b, t, c: (b * nt + t, 0)
    grid_spec = pltpu.PrefetchScalarGridSpec(
        num_scalar_prefetch=1,
        grid=(nb, nt),
        in_specs=[pl.BlockSpec(memory_space=pl.ANY),
                  pl.BlockSpec((COMBINE_TOKENS, ne), tok),
                  pl.BlockSpec((COMBINE_TOKENS, d), tok),
                  pl.BlockSpec((None, 1, d), lambda b, t, c: (b, 0, 0))],
        out_specs=pl.BlockSpec((COMBINE_TOKENS, d), tok),
        scratch_shapes=[pltpu.VMEM((2, pl.cdiv(ne * COMBINE_SLOTS_PER_EXPERT, COMBINE_CHUNK) * COMBINE_CHUNK, d), F32),
                        pltpu.VMEM((COMBINE_TOKENS, d), F32),
                        pltpu.VMEM((ne, COMBINE_TOKENS, LANES), I32),
                        pltpu.SemaphoreType.DMA((2,))],
    )
    return pl.pallas_call(
        functools.partial(_combine_kernel, ne=ne, n=n, cap=cap, nbound=nbound),
        grid_spec=grid_spec,
        out_shape=jax.ShapeDtypeStruct((m, d), F32),
        compiler_params=_cparams("arbitrary", "arbitrary"),
    )(cnt[:, :, :nbound].reshape(-1), y, pos_t, res, g2)


def _moe(h2, nb, g, sh, sc, gate2, w_router, w_gate, w_up, w_down, layer):
    m, d = h2.shape
    s = m // nb
    ne = w_router.shape[1]
    cap = EC_FACTOR * s // ne
    xp, aff_t = _router(h2, nb, g, sh, sc, w_router.T)
    pos, cnt, idx, gslot = _topk(aff_t, cap)
    rows = idx + (jnp.arange(nb, dtype=I32) * s)[:, None, None]
    rows = jnp.swapaxes(rows, 0, 1).reshape(ne, 1, nb * cap)
    gslot = jnp.swapaxes(gslot, 0, 1).reshape(ne, nb * cap, 1)
    hid = _expert_up(xp, rows, w_gate, w_up, layer)
    y = _expert_down(hid, w_down, gslot, layer).reshape(ne * nb * cap, d)
    pos_t = jnp.swapaxes(pos, 1, 2).reshape(m, ne)
    return _combine(cnt, y, pos_t, h2, gate2, nb, cap)


def _rope_tables(s):
    rows = s // GRID_W
    r, col = jnp.meshgrid(jnp.arange(rows, dtype=F32), jnp.arange(GRID_W, dtype=F32), indexing="ij")
    r, col = r.reshape(-1), col.reshape(-1)
    n_pairs = HEAD_DIM // 4
    inv = ROPE_BASE ** (-jnp.arange(n_pairs, dtype=F32) / n_pairs)
    ang_r = r[:, None] * inv[None, :]
    ang_c = col[:, None] * inv[None, :]
    ang = jnp.concatenate([ang_r, ang_r, ang_c, ang_c], axis=-1)
    cos, sin = jnp.cos(ang), jnp.sin(ang)
    first = (jnp.arange(HEAD_DIM) % (HEAD_DIM // 2)) < HEAD_DIM // 4
    sin_a = jnp.where(first, -sin, 0.0)
    sin_b = jnp.where(first, 0.0, sin)
    two = lambda a: jnp.concatenate([a, a], axis=-1)
    return two(cos), two(sin_a), two(sin_b)


def kernel(x, c, ctx, c_ctx, ada_w, ada_b, norm1_g, norm2_g, attn_w_q, attn_w_kv, attn_w_o, attn_q_gain, attn_k_gain, attn_sink, conv_w_pw1, conv_b_pw1, conv_w_dw, conv_b_dw, conv_ln_g, conv_ln_b, conv_w_pw2, conv_b_pw2, moe_router, moe_w_gate, moe_w_up, moe_w_down):
    nb, s, d = x.shape
    lc = ctx.shape[1]
    depth = ada_w.shape[0]
    assert depth == 2, "layer 0 attention, layer 1 convolution; the context stream is never updated"
    m = nb * s

    cin = jnp.concatenate([c, c_ctx[None, :], jnp.zeros((8 - nb - 1, d), F32)], axis=0)
    mod = _ada(cin, ada_w, ada_b)

    def mods(layer, row0, nrow):
        return [mod[layer, row0:row0 + nrow, k * d:(k + 1) * d].reshape(nrow, 1, d) for k in range(6)]

    h = x.reshape(m, d)
    two = lambda a: jnp.concatenate([a, a], axis=-1).reshape(1, LANES)

    sh1, sc1, g1, sh2, sc2, g2 = mods(0, 0, nb)
    sh1c, sc1c = mods(0, nb, 1)[:2]
    n1 = norm1_g[0].reshape(1, d)
    wq = attn_w_q[0].astype(BF)
    wkv = attn_w_kv[0].astype(BF)
    wo = attn_w_o[0].astype(BF)
    q_gain = two(attn_q_gain[0]) * (HEAD_DIM ** -0.5 * LOG2E)
    k_gain = two(attn_k_gain[0])
    q, k, v = _qkv_proj(h, nb, n1, sh1, sc1, wq, wkv, q_gain, k_gain, _rope_tables(s))
    kx, vx = _qkv_proj(ctx.reshape(nb * lc, d), nb, n1, sh1c, sc1c, None, wkv, None, k_gain, None)
    att = _attention(q, k, v, kx, vx, attn_sink[0])
    h = _mm_res(att, wo, h, g1, nb)
    h = _moe(h, nb, norm2_g[0].reshape(1, d), sh2, sc2, g2,
             moe_router[0], moe_w_gate, moe_w_up, moe_w_down, 0)

    sh1, sc1, g1, sh2, sc2, g2 = mods(1, 0, nb)
    u = _pw1_glu(h, nb, norm1_g[1].reshape(1, d), sh1, sc1,
                 conv_w_pw1[0].astype(BF), conv_b_pw1[0].reshape(1, 2 * d))
    h = _conv_pw2(u, nb, conv_w_dw[0], conv_b_dw[0].reshape(1, d), conv_ln_g[0].reshape(1, d),
                  conv_ln_b[0].reshape(1, d), conv_w_pw2[0].astype(BF), conv_b_pw2[0].reshape(1, d), h, g1)
    h = _moe(h, nb, norm2_g[1].reshape(1, d), sh2, sc2, g2,
             moe_router[1], moe_w_gate, moe_w_up, moe_w_down, 1)
    return h.reshape(nb, s, d)
```

```python
import functools

import jax
import jax.numpy as jnp
from jax import lax
from jax.experimental import pallas as pl
from jax.experimental.pallas import tpu as pltpu

F32 = jnp.float32
BF = jnp.bfloat16
I32 = jnp.int32
U32 = jnp.uint32

HEAD_DIM = 64
GROUP = 4
GRID_W = 64
ROPE_BASE = 10000.0
ATT_BLOCK = 128
CONV_WIDTH = 31
CONV_HALO = 16
CONV_ROWS = 64
SUBLANES = 8
N_EXPERTS = 16
EC_FACTOR = 2
EPS = 1e-6
NEG_INF = -1e30
LOG2E = 1.4426950408889634
LANES = 128
V7X_VMEM_LIMIT = 56 * 1024 * 1024


def _cparams(*sem):
    return pltpu.CompilerParams(dimension_semantics=sem, vmem_limit_bytes=V7X_VMEM_LIMIT)


def _dot(a, b):
    return jnp.dot(a, b, preferred_element_type=F32)


def _dot_nt(a, b):
    return lax.dot_general(a, b, (((1,), (1,)), ((), ())), preferred_element_type=F32)


def _split_bf16(x):
    hi = x.astype(BF)
    lo = (x - hi.astype(F32)).astype(BF)
    return hi, lo


def _norm_mod(x, g, sh, sc):
    ms = jnp.mean(x * x, axis=-1, keepdims=True)
    return (x * lax.rsqrt(ms + EPS) * g) * (1.0 + sc) + sh


def _ada_kernel(c_ref, w_ref, b_ref, o_ref):
    c = c_ref[...]
    s = c * jax.nn.sigmoid(c)
    o_ref[...] = _dot(s.astype(BF), w_ref[...].astype(BF)) + b_ref[...]


def _ada(cin, ada_w, ada_b):
    depth, d, n = ada_w.shape
    rows = cin.shape[0]
    tn = min(n, 1024)
    return pl.pallas_call(
        _ada_kernel,
        grid=(depth, n // tn),
        in_specs=[pl.BlockSpec((rows, d), lambda l, j: (0, 0)),
                  pl.BlockSpec((None, d, tn), lambda l, j: (l, 0, j)),
                  pl.BlockSpec((None, 1, tn), lambda l, j: (l, 0, j))],
        out_specs=pl.BlockSpec((None, rows, tn), lambda l, j: (l, 0, j)),
        out_shape=jax.ShapeDtypeStruct((depth, rows, n), F32),
        compiler_params=_cparams("arbitrary", "arbitrary"),
    )(cin, ada_w, ada_b.reshape(depth, 1, n))


def _head_norm(y, gain):
    w = y.shape[1]
    r = lax.broadcasted_iota(I32, (w, w), 0) // HEAD_DIM
    c = lax.broadcasted_iota(I32, (w, w), 1) // HEAD_DIM
    bd = jnp.where(r == c, 1.0 / HEAD_DIM, 0.0).astype(BF)
    hi, lo = _split_bf16(y * y)
    ms = _dot(hi, bd) + _dot(lo, bd)
    return y * lax.rsqrt(ms + EPS) * jnp.concatenate([gain] * (w // LANES), axis=1)


def _rope(y, cos, sin_a, sin_b):
    return (y * cos + pltpu.roll(y, LANES - HEAD_DIM // 4, 1) * sin_a
            + pltpu.roll(y, HEAD_DIM // 4, 1) * sin_b)


def _qkv_kernel(*refs, nh, nkv, rope, with_q):
    if with_q:
        (x_ref, g_ref, sh_ref, sc_ref, wq_ref, wkv_ref, qg_ref, kg_ref,
         cos_ref, sa_ref, sb_ref, q_ref, k_ref, v_ref) = refs
    else:
        (x_ref, g_ref, sh_ref, sc_ref, wkv_ref, kg_ref, k_ref, v_ref) = refs
    hb = _norm_mod(x_ref[...], g_ref[...], sh_ref[...], sc_ref[...]).astype(BF)
    if rope:
        cos, sin_a, sin_b = cos_ref[...], sa_ref[...], sb_ref[...]
    kvd = nkv * HEAD_DIM
    kv = _dot(hb, wkv_ref[...])
    def finish(y_wide, gain, out_ref, head0):
        yn = _head_norm(y_wide, gain)
        for part in range(y_wide.shape[1] // LANES):
            y = yn[:, part * LANES:(part + 1) * LANES]
            if rope:
                y = _rope(y, cos, sin_a, sin_b)
            yb = y.astype(BF)
            out_ref[head0 + 2 * part] = yb[:, :HEAD_DIM]
            out_ref[head0 + 2 * part + 1] = yb[:, HEAD_DIM:]

    cw = min(2 * LANES, kvd)
    per = cw // HEAD_DIM
    for c in range(kvd // cw):
        finish(kv[:, c * cw:(c + 1) * cw], kg_ref[...], k_ref, per * c)
    for h in range(nkv):
        v_ref[h] = kv[:, kvd + h * HEAD_DIM:kvd + (h + 1) * HEAD_DIM].astype(BF)
    if with_q:
        nq = nh * HEAD_DIM // cw
        ahead = _dot(hb, wq_ref[:, :cw])
        for c in range(nq):
            qc = ahead
            if c + 1 < nq:
                ahead = _dot(hb, wq_ref[:, (c + 1) * cw:(c + 2) * cw])
            finish(qc, qg_ref[...], q_ref, per * c)


def _qkv_proj(x2, nb, g, sh, sc, wq, wkv, qg, kg, tables):
    m, d = x2.shape
    s = m // nb
    nkv = wkv.shape[1] // (2 * HEAD_DIM)
    with_q = wq is not None
    tm = min(s, 256)
    nt = s // tm
    shared = sh.shape[0] == 1
    mod_map = (lambda b, i: (0, 0, 0)) if shared else (lambda b, i: (b, 0, 0))
    row_spec = pl.BlockSpec((tm, d), lambda b, i: (b * nt + i, 0))
    vec_d = pl.BlockSpec((1, d), lambda b, i: (0, 0))
    mod_spec = pl.BlockSpec((None, 1, d), mod_map)
    vec_l = pl.BlockSpec((1, LANES), lambda b, i: (0, 0))
    tab_spec = pl.BlockSpec((tm, LANES), lambda b, i: (i, 0))
    kv_spec = pl.BlockSpec((None, nkv, tm, HEAD_DIM), lambda b, i: (b, 0, i, 0))
    kv_shape = jax.ShapeDtypeStruct((nb, nkv, s, HEAD_DIM), BF)
    if with_q:
        nh = wq.shape[1] // HEAD_DIM
        ins = [x2, g, sh, sc, wq, wkv, qg, kg, *tables]
        in_specs = [row_spec, vec_d, mod_spec, mod_spec,
                    pl.BlockSpec(wq.shape, lambda b, i: (0, 0)),
                    pl.BlockSpec(wkv.shape, lambda b, i: (0, 0)),
                    vec_l, vec_l, tab_spec, tab_spec, tab_spec]
        out_specs = [pl.BlockSpec((None, nh, tm, HEAD_DIM), lambda b, i: (b, 0, i, 0)), kv_spec, kv_spec]
        out_shape = [jax.ShapeDtypeStruct((nb, nh, s, HEAD_DIM), BF), kv_shape, kv_shape]
    else:
        nh = 0
        ins = [x2, g, sh, sc, wkv, kg]
        in_specs = [row_spec, vec_d, mod_spec, mod_spec,
                    pl.BlockSpec(wkv.shape, lambda b, i: (0, 0)), vec_l]
        out_specs = [kv_spec, kv_spec]
        out_shape = [kv_shape, kv_shape]
    return pl.pallas_call(
        functools.partial(_qkv_kernel, nh=nh, nkv=nkv, rope=with_q, with_q=with_q),
        grid=(nb, nt), in_specs=in_specs, out_specs=out_specs, out_shape=out_shape,
        compiler_params=_cparams("arbitrary", "arbitrary"),
    )(*ins)


def _attn_kernel(sink_ref, q_ref, kp_ref, kc_ref, kn_ref, vp_ref, vc_ref, vn_ref,
                 kx_ref, vx_ref, o_ref, bp_ref, bn_ref, kcat_ref, vcat_ref, *, nkv, nblk):
    i = pl.program_id(1)
    rows = GROUP * ATT_BLOCK
    w = ATT_BLOCK
    for h in range(nkv):
        for part, (k_ref, v_ref) in enumerate(((kp_ref, vp_ref), (kc_ref, vc_ref), (kn_ref, vn_ref))):
            kcat_ref[h, part * w:(part + 1) * w] = k_ref[h]
            vcat_ref[h, part * w:(part + 1) * w] = v_ref[h]
        kcat_ref[h, 3 * w:] = kx_ref[h]
        vcat_ref[h, 3 * w:] = vx_ref[h]
    qi = lax.broadcasted_iota(I32, (rows, ATT_BLOCK), 0) % ATT_BLOCK
    kj = lax.broadcasted_iota(I32, (rows, ATT_BLOCK), 1)
    prev_ok = (kj >= qi) & (i > 0)
    next_ok = (kj <= qi) & (i < nblk - 1)
    bp_ref[...] = jnp.where(prev_ok, 0.0, NEG_INF)
    bn_ref[...] = jnp.where(next_ok, 0.0, NEG_INF)
    def scores(h):
        q = q_ref[GROUP * h:GROUP * (h + 1)].reshape(rows, HEAD_DIM)
        return _dot_nt(q, kcat_ref[h])

    ahead = scores(0)
    for h in range(nkv):
        raw = ahead
        if h + 1 < nkv:
            ahead = scores(h + 1)
        s = jnp.concatenate([raw[:, :w] + bp_ref[...], raw[:, w:2 * w], raw[:, 2 * w:3 * w] + bn_ref[...],
                             raw[:, 3 * w:]], axis=1)
        sink = jnp.concatenate(
            [jnp.full((ATT_BLOCK, 1), sink_ref[GROUP * h + g] * LOG2E, F32) for g in range(GROUP)], axis=0)
        m = jnp.maximum(jnp.max(s, axis=-1, keepdims=True), sink)
        p = jnp.exp2(s - m)
        den = jnp.sum(p, axis=-1, keepdims=True) + jnp.exp2(sink - m)
        o = (_dot(p.astype(BF), vcat_ref[h]) / den).astype(BF)
        for g in range(GROUP):
            hh = GROUP * h + g
            o_ref[:, hh * HEAD_DIM:(hh + 1) * HEAD_DIM] = o[g * ATT_BLOCK:(g + 1) * ATT_BLOCK]


def _attention(q, k, v, kx, vx, sink):
    nb, nh, s, _ = q.shape
    nkv = k.shape[1]
    lc = kx.shape[2]
    nblk = s // ATT_BLOCK
    blk = lambda f: pl.BlockSpec((None, nkv, ATT_BLOCK, HEAD_DIM), f)
    prev = lambda b, i: (b, 0, jnp.maximum(i - 1, 0), 0)
    cur = lambda b, i: (b, 0, i, 0)
    nxt = lambda b, i: (b, 0, jnp.minimum(i + 1, nblk - 1), 0)
    ctx_spec = pl.BlockSpec((None, nkv, lc, HEAD_DIM), lambda b, i: (b, 0, 0, 0))
    return pl.pallas_call(
        functools.partial(_attn_kernel, nkv=nkv, nblk=nblk),
        grid=(nb, nblk),
        in_specs=[pl.BlockSpec(memory_space=pltpu.SMEM),
                  pl.BlockSpec((None, nh, ATT_BLOCK, HEAD_DIM), cur),
                  blk(prev), blk(cur), blk(nxt), blk(prev), blk(cur), blk(nxt), ctx_spec, ctx_spec],
        out_specs=pl.BlockSpec((ATT_BLOCK, nh * HEAD_DIM), lambda b, i: (b * nblk + i, 0)),
        out_shape=jax.ShapeDtypeStruct((nb * s, nh * HEAD_DIM), BF),
        scratch_shapes=[pltpu.VMEM((GROUP * ATT_BLOCK, ATT_BLOCK), F32)] * 2
        + [pltpu.VMEM((nkv, 3 * ATT_BLOCK + lc, HEAD_DIM), BF)] * 2,
        compiler_params=_cparams("arbitrary", "arbitrary"),
    )(sink, q, k, k, k, v, v, v, kx, vx)


def _mm_res_kernel(a_ref, w_ref, res_ref, gate_ref, o_ref):
    o_ref[...] = res_ref[...] + gate_ref[...] * _dot(a_ref[...], w_ref[...])


def _mm_res(a, w, res, gate, nb):
    m, kd = a.shape
    n = w.shape[1]
    tm = min(m // nb, 512)
    per_b = m // nb // tm
    return pl.pallas_call(
        _mm_res_kernel,
        grid=(m // tm,),
        in_specs=[pl.BlockSpec((tm, kd), lambda i: (i, 0)),
                  pl.BlockSpec((kd, n), lambda i: (0, 0)),
                  pl.BlockSpec((tm, n), lambda i: (i, 0)),
                  pl.BlockSpec((None, 1, n), lambda i: (i // per_b, 0, 0))],
        out_specs=pl.BlockSpec((tm, n), lambda i: (i, 0)),
        out_shape=jax.ShapeDtypeStruct((m, n), F32),
        compiler_params=_cparams("arbitrary"),
    )(a, w, res, gate)


def _pw1_kernel(x_ref, g_ref, sh_ref, sc_ref, wa_ref, wg_ref, ba_ref, bg_ref, o_ref):
    hb = _norm_mod(x_ref[...], g_ref[...], sh_ref[...], sc_ref[...]).astype(BF)
    a = _dot(hb, wa_ref[...]) + ba_ref[...]
    gt = _dot(hb, wg_ref[...]) + bg_ref[...]
    o_ref[...] = a * jax.nn.sigmoid(gt)


def _pw1_glu(x2, nb, g, sh, sc, w, bias):
    m, d = x2.shape
    tm = min(m // nb, 512)
    per_b = m // nb // tm
    tn = min(d, 1024)
    nj = d // tn
    mod_spec = pl.BlockSpec((None, 1, d), lambda j, i: (i // per_b, 0, 0))
    return pl.pallas_call(
        _pw1_kernel,
        grid=(nj, m // tm),
        in_specs=[pl.BlockSpec((tm, d), lambda j, i: (i, 0)),
                  pl.BlockSpec((1, d), lambda j, i: (0, 0)), mod_spec, mod_spec,
                  pl.BlockSpec((d, tn), lambda j, i: (0, j)),
                  pl.BlockSpec((d, tn), lambda j, i: (0, nj + j)),
                  pl.BlockSpec((1, tn), lambda j, i: (0, j)),
                  pl.BlockSpec((1, tn), lambda j, i: (0, nj + j))],
        out_specs=pl.BlockSpec((tm, tn), lambda j, i: (i, j)),
        out_shape=jax.ShapeDtypeStruct((m, d), F32),
        compiler_params=_cparams("arbitrary", "arbitrary"),
    )(x2, g, sh, sc, w, w, bias, bias)


def _conv_kernel(up_ref, uc_ref, un_ref, wdw_ref, bdw_ref, lg_ref, lb_ref, w2_ref, b2_ref,
                 res_ref, gate_ref, o_ref, ext_ref, cv_ref, *, nt):
    i = pl.program_id(1)
    tm, d = uc_ref.shape
    zero = jnp.zeros((CONV_HALO, d), F32)
    ext_ref[0:CONV_HALO, :] = jnp.where(i > 0, up_ref[...], zero)
    ext_ref[CONV_HALO:CONV_HALO + tm, :] = uc_ref[...]
    ext_ref[CONV_HALO + tm:, :] = jnp.where(i < nt - 1, un_ref[...], zero)
    off = CONV_HALO - CONV_WIDTH // 2
    rb = min(tm, CONV_ROWS)
    span = rb + 2 * CONV_HALO
    for c in range(d // LANES):
        cols = slice(c * LANES, (c + 1) * LANES)
        wcol = wdw_ref[:, cols]
        for r0 in range(0, tm, rb):
            xin = ext_ref[r0:r0 + span, cols]
            shifted = {0: xin}
            acc = jnp.broadcast_to(bdw_ref[:, cols], (rb, LANES))
            for k in range(CONV_WIDTH):
                r, a = (off + k) % SUBLANES, (off + k) // SUBLANES
                if r not in shifted:
                    shifted[r] = pltpu.roll(xin, span - r, 0)
                acc = acc + wcol[k:k + 1] * shifted[r][SUBLANES * a:SUBLANES * a + rb]
            cv_ref[r0:r0 + rb, cols] = acc
    u = cv_ref[...]
    mu = jnp.mean(u, axis=-1, keepdims=True)
    var = jnp.mean(jnp.square(u - mu), axis=-1, keepdims=True)
    y = (u - mu) * lax.rsqrt(var + EPS) * lg_ref[...] + lb_ref[...]
    y = y * jax.nn.sigmoid(y)
    out = _dot(y.astype(BF), w2_ref[...]) + b2_ref[...]
    o_ref[...] = res_ref[...] + gate_ref[...] * out


def _conv_pw2(u, nb, wdw, bdw, lg, lb, w2, b2, res, gate):
    m, d = u.shape
    s = m // nb
    tm = min(s, 256)
    nt = s // tm
    hb = tm // CONV_HALO
    nhb = s // CONV_HALO
    vec = pl.BlockSpec((1, d), lambda b, i: (0, 0))
    tile = pl.BlockSpec((tm, d), lambda b, i: (b * nt + i, 0))
    return pl.pallas_call(
        functools.partial(_conv_kernel, nt=nt),
        grid=(nb, nt),
        in_specs=[pl.BlockSpec((CONV_HALO, d), lambda b, i: (b * nhb + jnp.maximum(i * hb - 1, 0), 0)),
                  tile,
                  pl.BlockSpec((CONV_HALO, d), lambda b, i: (b * nhb + jnp.minimum((i + 1) * hb, nhb - 1), 0)),
                  pl.BlockSpec((CONV_WIDTH, d), lambda b, i: (0, 0)),
                  vec, vec, vec,
                  pl.BlockSpec((d, d), lambda b, i: (0, 0)),
                  vec, tile,
                  pl.BlockSpec((None, 1, d), lambda b, i: (b, 0, 0))],
        out_specs=tile,
        out_shape=jax.ShapeDtypeStruct((m, d), F32),
        scratch_shapes=[pltpu.VMEM((tm + 2 * CONV_HALO, d), F32), pltpu.VMEM((tm, d), F32)],
        compiler_params=_cparams("arbitrary", "arbitrary"),
    )(u, u, u, wdw, bdw, lg, lb, w2, b2, res, gate)


def _router_kernel(x_ref, g_ref, sh_ref, sc_ref, wr_ref, xp_ref, aff_ref):
    hn = _norm_mod(x_ref[...], g_ref[...], sh_ref[...], sc_ref[...])
    h_hi, h_lo = _split_bf16(hn)
    w_hi, w_lo = _split_bf16(wr_ref[...])
    logits = _dot_nt(w_hi, h_hi) + _dot_nt(w_hi, h_lo) + _dot_nt(w_lo, h_hi)
    mx = jnp.max(logits, axis=0, keepdims=True)
    ex = jnp.exp(logits - mx)
    aff_ref[...] = ex / jnp.sum(ex, axis=0, keepdims=True)
    bits = lax.bitcast_convert_type(h_hi.astype(F32), U32)
    tm = bits.shape[0]
    d2 = bits.shape[1] // 2
    word = bits[:, :d2] | (bits[:, d2:] >> 16)
    tok_rows = d2 // LANES
    for j in range(tok_rows):
        xp_ref[pl.ds(j, tm, stride=tok_rows), :] = word[:, j * LANES:(j + 1) * LANES]


def _router(x2, nb, g, sh, sc, wr_t):
    m, d = x2.shape
    s = m // nb
    ne = wr_t.shape[0]
    tm = min(s, 512)
    nt = s // tm
    tok_rows = d // 2 // LANES
    mod_spec = pl.BlockSpec((None, 1, d), lambda b, i: (b, 0, 0))
    return pl.pallas_call(
        _router_kernel,
        grid=(nb, nt),
        in_specs=[pl.BlockSpec((tm, d), lambda b, i: (b * nt + i, 0)),
                  pl.BlockSpec((1, d), lambda b, i: (0, 0)), mod_spec, mod_spec,
                  pl.BlockSpec((ne, d), lambda b, i: (0, 0))],
        out_specs=[pl.BlockSpec((tm * tok_rows, LANES), lambda b, i: (b * nt + i, 0)),
                   pl.BlockSpec((None, ne, tm), lambda b, i: (b, 0, i))],
        out_shape=[jax.ShapeDtypeStruct((m * tok_rows, LANES), U32),
                   jax.ShapeDtypeStruct((nb, ne, s), F32)],
        compiler_params=_cparams("arbitrary", "arbitrary"),
    )(x2, g, sh, sc, wr_t)


def _prefix_incl(mask_bf):
    ne, s = mask_bf.shape
    r = lax.broadcasted_iota(I32, (LANES, LANES), 0)
    c = lax.broadcasted_iota(I32, (LANES, LANES), 1)
    tri = jnp.where(r <= c, 1.0, 0.0).astype(BF)
    carry = jnp.zeros((ne, 1), F32)
    out = []
    for ch in range(s // LANES):
        inc = _dot(mask_bf[:, ch * LANES:(ch + 1) * LANES], tri) + carry
        out.append(inc)
        carry = inc[:, LANES - 1:LANES]
    return jnp.concatenate(out, axis=1)


def _topk_kernel(aff_ref, pos_ref, cnt_ref, idx_ref, gslot_ref, pos_scr, *, cap):
    aff = aff_ref[...]
    ne, s = aff.shape
    v = lax.bitcast_convert_type(aff, I32)

    def bit_step(it, prefix):
        cand = prefix | jnp.left_shift(jnp.int32(1), 30 - it)
        cnt = jnp.sum(jnp.where(v >= cand, 1.0, 0.0), axis=1, keepdims=True)
        return jnp.where(cnt >= cap, cand, prefix)

    thr = lax.fori_loop(0, 31, bit_step, jnp.zeros((ne, 1), I32))
    gt = v > thr
    eq = v == thr
    need = cap - jnp.sum(jnp.where(gt, 1.0, 0.0), axis=1, keepdims=True)
    eq_bf = jnp.where(eq, 1.0, 0.0).astype(BF)
    eq_rank = _prefix_incl(eq_bf) - eq_bf.astype(F32)
    sel = gt | (eq & (eq_rank < need))
    sel_bf = jnp.where(sel, 1.0, 0.0).astype(BF)
    incl = _prefix_incl(sel_bf)
    pos = jnp.where(sel, incl - 1.0, -1.0).astype(I32)
    pos_ref[...] = pos
    t_i = lax.broadcasted_iota(I32, (s, LANES), 0)
    j_i = lax.broadcasted_iota(I32, (s, LANES), 1)
    before = jnp.where(t_i < j_i * LANES, 1.0, 0.0).astype(BF)
    cnt_ref[...] = _dot(sel_bf, before).astype(I32)
    pos_scr[...] = pos
    t_row = lax.broadcasted_iota(I32, (SUBLANES, s), 1)
    r_row = lax.broadcasted_iota(I32, (SUBLANES, s), 0)
    tval = jnp.where(r_row == 0, t_row // 64, jnp.where(r_row == 1, t_row % 64, 0)).astype(F32)
    slot = lax.broadcasted_iota(I32, (cap, 1), 0)

    def expert_step(e, carry):
        a = aff_ref[pl.ds(e, 1), :]
        a_hi = a.astype(BF).astype(F32)
        a_mid = (a - a_hi).astype(BF).astype(F32)
        a_lo = a - a_hi - a_mid
        vals = jnp.where(r_row == 2, a_hi, jnp.where(r_row == 3, a_mid, jnp.where(r_row == 4, a_lo, tval)))
        onehot = jnp.where(pos_scr[pl.ds(e, 1), :] == slot, 1.0, 0.0).astype(BF)
        res = _dot_nt(vals.astype(BF), onehot)
        idx_ref[pl.ds(e, 1), :] = (res[0:1] * 64.0 + res[1:2]).astype(I32)
        gslot_ref[pl.ds(e, 1), :] = res[2:3] + res[3:4] + res[4:5]
        return carry

    lax.fori_loop(0, ne, expert_step, 0)


def _topk(aff_t, cap):
    nb, ne, s = aff_t.shape
    blk = pl.BlockSpec((None, ne, s), lambda b: (b, 0, 0))
    return pl.pallas_call(
        functools.partial(_topk_kernel, cap=cap),
        grid=(nb,),
        in_specs=[blk],
        out_specs=[blk,
                   pl.BlockSpec((None, ne, LANES), lambda b: (b, 0, 0)),
                   pl.BlockSpec((None, ne, cap), lambda b: (b, 0, 0)),
                   pl.BlockSpec((None, ne, cap), lambda b: (b, 0, 0))],
        out_shape=[jax.ShapeDtypeStruct((nb, ne, s), I32),
                   jax.ShapeDtypeStruct((nb, ne, LANES), I32),
                   jax.ShapeDtypeStruct((nb, ne, cap), I32),
                   jax.ShapeDtypeStruct((nb, ne, cap), F32)],
        scratch_shapes=[pltpu.VMEM((ne, s), I32)],
        compiler_params=_cparams("arbitrary"),
    )(aff_t)


GATHER_UNROLL = 8


def _token_copy(src_ref, src_tok, dst_ref, dst_tok, tok_rows, sem):
    src = pl.multiple_of(src_tok * tok_rows, tok_rows)
    dst = pl.multiple_of(dst_tok * tok_rows, tok_rows)
    return pltpu.make_async_copy(src_ref.at[pl.ds(src, tok_rows), :], dst_ref.at[pl.ds(dst, tok_rows), :], sem)


def _expert_up_kernel(cur_ref, nxt_ref, xp_ref, wg_ref, wu_ref, o_ref, raw_ref, xb_ref, sem, *, ne, n, nj):
    e = pl.program_id(0)
    j = pl.program_id(1)
    slot = e % 2
    per = n // nj
    tok_rows = raw_ref.shape[1] // n

    def issue(idx_ref, dst, lo, cnt):
        def body(k, carry):
            for i in range(GATHER_UNROLL):
                c = lo + k * GATHER_UNROLL + i
                _token_copy(xp_ref, idx_ref[0, c], raw_ref.at[dst], c, tok_rows, sem.at[dst]).start(priority=i % 2)
            return carry

        lax.fori_loop(0, cnt // GATHER_UNROLL, body, 0)

    @pl.when((e == 0) & (j == 0))
    def _():
        issue(cur_ref, 0, 0, n)

    def drain(which):
        def body(c, carry):
            _token_copy(xp_ref, 0, raw_ref.at[which], c, tok_rows, sem.at[which]).wait()
            return carry

        lax.fori_loop(0, n, body, 0, unroll=8)

    @pl.when(j == 0)
    def _():
        drain(slot)
        d2 = tok_rows * LANES
        for q in range(tok_rows):
            w = raw_ref[slot, pl.ds(q, n, stride=tok_rows), :]
            xb_ref[:, q * LANES:(q + 1) * LANES] = lax.bitcast_convert_type(
                w & jnp.uint32(0xFFFF0000), F32).astype(BF)
            xb_ref[:, d2 + q * LANES:d2 + (q + 1) * LANES] = lax.bitcast_convert_type(w << 16, F32).astype(BF)

    for i in range(per):
        c = j * per + i
        _token_copy(xp_ref, nxt_ref[0, c], raw_ref.at[1 - slot], c, tok_rows, sem.at[1 - slot]).start(priority=i % 2)

    x = xb_ref[...]
    g = _dot(x, wg_ref[...].astype(BF))
    u = _dot(x, wu_ref[...].astype(BF))
    o_ref[...] = (g * jax.nn.sigmoid(g) * u).astype(BF)

    @pl.when((e == ne - 1) & (j == nj - 1))
    def _():
        drain(1 - slot)


def _expert_up(xp, rows, w_gate, w_up, layer):
    _, ne, d, f = w_gate.shape
    n = rows.shape[2]
    tf = min(f, 256)
    nj = f // tf
    tok_rows = d // 2 // LANES
    w_spec = pl.BlockSpec((None, None, d, tf), lambda e, j: (layer, e, 0, j))
    idx_spec = lambda f_: pl.BlockSpec((None, 1, n), f_, memory_space=pltpu.SMEM)
    return pl.pallas_call(
        functools.partial(_expert_up_kernel, ne=ne, n=n, nj=nj),
        grid=(ne, nj),
        in_specs=[idx_spec(lambda e, j: (e, 0, 0)),
                  idx_spec(lambda e, j: (jnp.minimum(e + 1, ne - 1), 0, 0)),
                  pl.BlockSpec(memory_space=pl.ANY), w_spec, w_spec],
        out_specs=pl.BlockSpec((None, n, tf), lambda e, j: (e, 0, j)),
        out_shape=jax.ShapeDtypeStruct((ne, n, f), BF),
        scratch_shapes=[pltpu.VMEM((2, n * tok_rows, LANES), U32), pltpu.VMEM((n, d), BF),
                        pltpu.SemaphoreType.DMA((2,))],
        compiler_params=_cparams("arbitrary", "arbitrary"),
    )(rows, rows, xp, w_gate, w_up)


def _expert_down_kernel(h_ref, w_ref, gs_ref, o_ref):
    o_ref[...] = _dot(h_ref[...], w_ref[...].astype(BF)) * gs_ref[...]


def _expert_down(hid, w_down, gslot, layer):
    ne, n, f = hid.shape
    d = w_down.shape[3]
    tn = min(d, 512)
    return pl.pallas_call(
        _expert_down_kernel,
        grid=(ne, d // tn),
        in_specs=[pl.BlockSpec((None, n, f), lambda e, j: (e, 0, 0)),
                  pl.BlockSpec((None, None, f, tn), lambda e, j: (layer, e, 0, j)),
                  pl.BlockSpec((None, n, 1), lambda e, j: (e, 0, 0))],
        out_specs=pl.BlockSpec((None, n, tn), lambda e, j: (e, 0, j)),
        out_shape=jax.ShapeDtypeStruct((ne, n, d), F32),
        compiler_params=_cparams("arbitrary", "arbitrary"),
    )(hid, w_down, gslot)


COMBINE_TOKENS = 128
COMBINE_CHUNK = 256
SLAB_ROWS = (128, 64, 32, 16, 8)
COMBINE_SLOTS_PER_EXPERT = COMBINE_TOKENS + SUBLANES


def _slab_copy(y_ref, src_row, buf_ref, dst_row, rows, sem):
    return pltpu.make_async_copy(y_ref.at[pl.ds(src_row, rows), :], buf_ref.at[pl.ds(dst_row, rows), :], sem)


def _combine_kernel(cnt_ref, y_ref, pos_ref, res_ref, g2_ref, o_ref, buf_ref, acc_ref, slot_ref, sem,
                    *, ne, n, cap, nbound):
    b = pl.program_id(0)
    t = pl.program_id(1)
    nt = nbound - 1
    step = b * nt + t
    last = pl.num_programs(0) * nt - 1

    @pl.when(step == 0)
    def _():
        buf_ref[...] = jnp.zeros_like(buf_ref)

    def plan(bq, tq):
        slabs, shift = [], []
        total = jnp.int32(0)
        for e in range(ne):
            base = (bq * ne + e) * nbound + tq
            c0 = cnt_ref[base]
            c1 = cnt_ref[base + 1]
            a0 = (c0 // SUBLANES) * SUBLANES
            run = jnp.where(c1 > c0, (c1 + SUBLANES - 1) // SUBLANES * SUBLANES - a0, 0)
            row0 = e * n + bq * cap + a0
            done = jnp.int32(0)
            for rows in SLAB_ROWS:
                take = (run & rows) != 0
                slabs.append((take, rows, pl.multiple_of(row0 + done, SUBLANES),
                              pl.multiple_of(total + done, SUBLANES)))
                done = done + jnp.where(take, rows, 0)
            shift.append(total - a0)
            total = total + run
        return slabs, shift, total

    def start_all(slabs, half):
        for take, rows, src, dst in slabs:
            @pl.when(take)
            def _(rows=rows, src=src, dst=dst):
                _slab_copy(y_ref, src, buf_ref.at[half], dst, rows, sem.at[half]).start()

    half = step % 2
    slabs, shift, total = plan(b, t)

    @pl.when(step == 0)
    def _():
        start_all(slabs, half)

    nxt = jnp.minimum(step + 1, last)
    nxt_slabs, _, _ = plan(nxt // nt, nxt % nt)

    @pl.when(step < last)
    def _():
        start_all(nxt_slabs, 1 - half)

    for take, rows, _, _ in slabs:
        @pl.when(take)
        def _(rows=rows):
            _slab_copy(y_ref, 0, buf_ref.at[half], 0, rows, sem.at[half]).wait()

    acc_ref[...] = jnp.zeros_like(acc_ref)
    pos = pos_ref[...]
    for e in range(ne):
        pe = pos[:, e:e + 1]
        where_e = jnp.where(pe >= 0, pe + shift[e], -1)
        slot_ref[e] = jnp.broadcast_to(where_e, (COMBINE_TOKENS, LANES))

    def chunk(ci, carry):
        start = pl.multiple_of(ci * COMBINE_CHUNK, COMBINE_CHUNK)
        parts = []
        for q in range(COMBINE_CHUNK // LANES):
            lane = lax.broadcasted_iota(I32, (COMBINE_TOKENS, LANES), 1) + (start + q * LANES)
            hot = jnp.zeros((COMBINE_TOKENS, LANES), F32)
            for e in range(ne):
                hot = jnp.where(slot_ref[e] == lane, 1.0, hot)
            parts.append(hot.astype(BF))
        onehot = jnp.concatenate(parts, axis=1)
        live = lax.broadcasted_iota(I32, (COMBINE_CHUNK, 1), 0) + start < total
        rows = jnp.where(live, buf_ref[half, pl.ds(start, COMBINE_CHUNK), :], 0.0)
        r_hi, r_lo = _split_bf16(rows)
        acc_ref[...] += _dot(onehot, r_hi) + _dot(onehot, r_lo)
        return carry

    lax.fori_loop(0, (total + COMBINE_CHUNK - 1) // COMBINE_CHUNK, chunk, 0)
    o_ref[...] = res_ref[...] + g2_ref[...] * acc_ref[...]


def _combine(cnt, y, pos_t, res, g2, nb, cap):
    m, d = res.shape
    s = m // nb
    ne = pos_t.shape[1]
    n = nb * cap
    nt = s // COMBINE_TOKENS
    nbound = nt + 1
    tok = lambda b, t, c: (b * nt + t, 0)
    grid_spec = pltpu.PrefetchScalarGridSpec(
        num_scalar_prefetch=1,
        grid=(nb, nt),
        in_specs=[pl.BlockSpec(memory_space=pl.ANY),
                  pl.BlockSpec((COMBINE_TOKENS, ne), tok),
                  pl.BlockSpec((COMBINE_TOKENS, d), tok),
                  pl.BlockSpec((None, 1, d), lambda b, t, c: (b, 0, 0))],
        out_specs=pl.BlockSpec((COMBINE_TOKENS, d), tok),
        scratch_shapes=[pltpu.VMEM((2, pl.cdiv(ne * COMBINE_SLOTS_PER_EXPERT, COMBINE_CHUNK) * COMBINE_CHUNK, d), F32),
                        pltpu.VMEM((COMBINE_TOKENS, d), F32),
                        pltpu.VMEM((ne, COMBINE_TOKENS, LANES), I32),
                        pltpu.SemaphoreType.DMA((2,))],
    )
    return pl.pallas_call(
        functools.partial(_combine_kernel, ne=ne, n=n, cap=cap, nbound=nbound),
        grid_spec=grid_spec,
        out_shape=jax.ShapeDtypeStruct((m, d), F32),
        compiler_params=_cparams("arbitrary", "arbitrary"),
    )(cnt[:, :, :nbound].reshape(-1), y, pos_t, res, g2)


def _moe(h2, nb, g, sh, sc, gate2, w_router, w_gate, w_up, w_down, layer):
    m, d = h2.shape
    s = m // nb
    ne = w_router.shape[1]
    cap = EC_FACTOR * s // ne
    xp, aff_t = _router(h2, nb, g, sh, sc, w_router.T)
    pos, cnt, idx, gslot = _topk(aff_t, cap)
    rows = idx + (jnp.arange(nb, dtype=I32) * s)[:, None, None]
    rows = jnp.swapaxes(rows, 0, 1).reshape(ne, 1, nb * cap)
    gslot = jnp.swapaxes(gslot, 0, 1).reshape(ne, nb * cap, 1)
    hid = _expert_up(xp, rows, w_gate, w_up, layer)
    y = _expert_down(hid, w_down, gslot, layer).reshape(ne * nb * cap, d)
    pos_t = jnp.swapaxes(pos, 1, 2).reshape(m, ne)
    return _combine(cnt, y, pos_t, h2, gate2, nb, cap)


def _rope_tables(s):
    rows = s // GRID_W
    r, col = jnp.meshgrid(jnp.arange(rows, dtype=F32), jnp.arange(GRID_W, dtype=F32), indexing="ij")
    r, col = r.reshape(-1), col.reshape(-1)
    n_pairs = HEAD_DIM // 4
    inv = ROPE_BASE ** (-jnp.arange(n_pairs, dtype=F32) / n_pairs)
    ang_r = r[:, None] * inv[None, :]
    ang_c = col[:, None] * inv[None, :]
    ang = jnp.concatenate([ang_r, ang_r, ang_c, ang_c], axis=-1)
    cos, sin = jnp.cos(ang), jnp.sin(ang)
    first = (jnp.arange(HEAD_DIM) % (HEAD_DIM // 2)) < HEAD_DIM // 4
    sin_a = jnp.where(first, -sin, 0.0)
    sin_b = jnp.where(first, 0.0, sin)
    two = lambda a: jnp.concatenate([a, a], axis=-1)
    return two(cos), two(sin_a), two(sin_b)


def kernel(x, c, ctx, c_ctx, ada_w, ada_b, norm1_g, norm2_g, attn_w_q, attn_w_kv, attn_w_o, attn_q_gain, attn_k_gain, attn_sink, conv_w_pw1, conv_b_pw1, conv_w_dw, conv_b_dw, conv_ln_g, conv_ln_b, conv_w_pw2, conv_b_pw2, moe_router, moe_w_gate, moe_w_up, moe_w_down):
    nb, s, d = x.shape
    lc = ctx.shape[1]
    depth = ada_w.shape[0]
    assert depth == 2, "layer 0 attention, layer 1 convolution; the context stream is never updated"
    m = nb * s

    cin = jnp.concatenate([c, c_ctx[None, :], jnp.zeros((8 - nb - 1, d), F32)], axis=0)
    mod = _ada(cin, ada_w, ada_b)

    def mods(layer, row0, nrow):
        return [mod[layer, row0:row0 + nrow, k * d:(k + 1) * d].reshape(nrow, 1, d) for k in range(6)]

    h = x.reshape(m, d)
    two = lambda a: jnp.concatenate([a, a], axis=-1).reshape(1, LANES)

    sh1, sc1, g1, sh2, sc2, g2 = mods(0, 0, nb)
    sh1c, sc1c = mods(0, nb, 1)[:2]
    n1 = norm1_g[0].reshape(1, d)
    wq = attn_w_q[0].astype(BF)
    wkv = attn_w_kv[0].astype(BF)
    wo = attn_w_o[0].astype(BF)
    q_gain = two(attn_q_gain[0]) * (HEAD_DIM ** -0.5 * LOG2E)
    k_gain = two(attn_k_gain[0])
    q, k, v = _qkv_proj(h, nb, n1, sh1, sc1, wq, wkv, q_gain, k_gain, _rope_tables(s))
    kx, vx = _qkv_proj(ctx.reshape(nb * lc, d), nb, n1, sh1c, sc1c, None, wkv, None, k_gain, None)
    att = _attention(q, k, v, kx, vx, attn_sink[0])
    h = _mm_res(att, wo, h, g1, nb)
    h = _moe(h, nb, norm2_g[0].reshape(1, d), sh2, sc2, g2,
             moe_router[0], moe_w_gate, moe_w_up, moe_w_down, 0)

    sh1, sc1, g1, sh2, sc2, g2 = mods(1, 0, nb)
    u = _pw1_glu(h, nb, norm1_g[1].reshape(1, d), sh1, sc1,
                 conv_w_pw1[0].astype(BF), conv_b_pw1[0].reshape(1, 2 * d))
    h = _conv_pw2(u, nb, conv_w_dw[0], conv_b_dw[0].reshape(1, d), conv_ln_g[0].reshape(1, d),
                  conv_ln_b[0].reshape(1, d), conv_w_pw2[0].astype(BF), conv_b_pw2[0].reshape(1, d), h, g1)
    h = _moe(h, nb, norm2_g[1].reshape(1, d), sh2, sc2, g2,
             moe_router[1], moe_w_gate, moe_w_up, moe_w_down, 1)
    return h.reshape(nb, s, d)
```

```python
import functools

import jax
import jax.numpy as jnp
from jax import lax
from jax.experimental import pallas as pl
from jax.experimental.pallas import tpu as pltpu

F32 = jnp.float32
BF = jnp.bfloat16
I32 = jnp.int32
U32 = jnp.uint32

HEAD_DIM = 64
GROUP = 4
GRID_W = 64
ROPE_BASE = 10000.0
ATT_BLOCK = 128
CONV_WIDTH = 31
CONV_HALO = 16
CONV_ROWS = 64
SUBLANES = 8
N_EXPERTS = 16
EC_FACTOR = 2
EPS = 1e-6
NEG_INF = -1e30
LOG2E = 1.4426950408889634
LANES = 128
V7X_VMEM_LIMIT = 56 * 1024 * 1024


def _cparams(*sem):
    return pltpu.CompilerParams(dimension_semantics=sem, vmem_limit_bytes=V7X_VMEM_LIMIT)


def _dot(a, b):
    return jnp.dot(a, b, preferred_element_type=F32)


def _dot_nt(a, b):
    return lax.dot_general(a, b, (((1,), (1,)), ((), ())), preferred_element_type=F32)


def _split_bf16(x):
    hi = x.astype(BF)
    lo = (x - hi.astype(F32)).astype(BF)
    return hi, lo


def _norm_mod(x, g, sh, sc):
    ms = jnp.mean(x * x, axis=-1, keepdims=True)
    return (x * lax.rsqrt(ms + EPS) * g) * (1.0 + sc) + sh


def _ada_kernel(c_ref, w_ref, b_ref, o_ref):
    c = c_ref[...]
    s = c * jax.nn.sigmoid(c)
    o_ref[...] = _dot(s.astype(BF), w_ref[...].astype(BF)) + b_ref[...]


def _ada(cin, ada_w, ada_b):
    depth, d, n = ada_w.shape
    rows = cin.shape[0]
    tn = min(n, 1024)
    return pl.pallas_call(
        _ada_kernel,
        grid=(depth, n // tn),
        in_specs=[pl.BlockSpec((rows, d), lambda l, j: (0, 0)),
                  pl.BlockSpec((None, d, tn), lambda l, j: (l, 0, j)),
                  pl.BlockSpec((None, 1, tn), lambda l, j: (l, 0, j))],
        out_specs=pl.BlockSpec((None, rows, tn), lambda l, j: (l, 0, j)),
        out_shape=jax.ShapeDtypeStruct((depth, rows, n), F32),
        compiler_params=_cparams("arbitrary", "arbitrary"),
    )(cin, ada_w, ada_b.reshape(depth, 1, n))


def _head_norm(y, gain):
    w = y.shape[1]
    r = lax.broadcasted_iota(I32, (w, w), 0) // HEAD_DIM
    c = lax.broadcasted_iota(I32, (w, w), 1) // HEAD_DIM
    bd = jnp.where(r == c, 1.0 / HEAD_DIM, 0.0).astype(BF)
    hi, lo = _split_bf16(y * y)
    ms = _dot(hi, bd) + _dot(lo, bd)
    return y * lax.rsqrt(ms + EPS) * jnp.concatenate([gain] * (w // LANES), axis=1)


def _rope(y, cos, sin_a, sin_b):
    return (y * cos + pltpu.roll(y, LANES - HEAD_DIM // 4, 1) * sin_a
            + pltpu.roll(y, HEAD_DIM // 4, 1) * sin_b)


def _qkv_kernel(*refs, nh, nkv, rope, with_q):
    if with_q:
        (x_ref, g_ref, sh_ref, sc_ref, wq_ref, wkv_ref, qg_ref, kg_ref,
         cos_ref, sa_ref, sb_ref, q_ref, k_ref, v_ref) = refs
    else:
        (x_ref, g_ref, sh_ref, sc_ref, wkv_ref, kg_ref, k_ref, v_ref) = refs
    hb = _norm_mod(x_ref[...], g_ref[...], sh_ref[...], sc_ref[...]).astype(BF)
    if rope:
        cos, sin_a, sin_b = cos_ref[...], sa_ref[...], sb_ref[...]
    kvd = nkv * HEAD_DIM
    kv = _dot(hb, wkv_ref[...])
    def finish(y_wide, gain, out_ref, head0):
        yn = _head_norm(y_wide, gain)
        for part in range(y_wide.shape[1] // LANES):
            y = yn[:, part * LANES:(part + 1) * LANES]
            if rope:
                y = _rope(y, cos, sin_a, sin_b)
            yb = y.astype(BF)
            out_ref[head0 + 2 * part] = yb[:, :HEAD_DIM]
            out_ref[head0 + 2 * part + 1] = yb[:, HEAD_DIM:]

    cw = min(2 * LANES, kvd)
    per = cw // HEAD_DIM
    for c in range(kvd // cw):
        finish(kv[:, c * cw:(c + 1) * cw], kg_ref[...], k_ref, per * c)
    for h in range(nkv):
        v_ref[h] = kv[:, kvd + h * HEAD_DIM:kvd + (h + 1) * HEAD_DIM].astype(BF)
    if with_q:
        nq = nh * HEAD_DIM // cw
        ahead = _dot(hb, wq_ref[:, :cw])
        for c in range(nq):
            qc = ahead
            if c + 1 < nq:
                ahead = _dot(hb, wq_ref[:, (c + 1) * cw:(c + 2) * cw])
            finish(qc, qg_ref[...], q_ref, per * c)


def _qkv_proj(x2, nb, g, sh, sc, wq, wkv, qg, kg, tables):
    m, d = x2.shape
    s = m // nb
    nkv = wkv.shape[1] // (2 * HEAD_DIM)
    with_q = wq is not None
    tm = min(s, 256)
    nt = s // tm
    shared = sh.shape[0] == 1
    mod_map = (lambda b, i: (0, 0, 0)) if shared else (lambda b, i: (b, 0, 0))
    row_spec = pl.BlockSpec((tm, d), lambda b, i: (b * nt + i, 0))
    vec_d = pl.BlockSpec((1, d), lambda b, i: (0, 0))
    mod_spec = pl.BlockSpec((None, 1, d), mod_map)
    vec_l = pl.BlockSpec((1, LANES), lambda b, i: (0, 0))
    tab_spec = pl.BlockSpec((tm, LANES), lambda b, i: (i, 0))
    kv_spec = pl.BlockSpec((None, nkv, tm, HEAD_DIM), lambda b, i: (b, 0, i, 0))
    kv_shape = jax.ShapeDtypeStruct((nb, nkv, s, HEAD_DIM), BF)
    if with_q:
        nh = wq.shape[1] // HEAD_DIM
        ins = [x2, g, sh, sc, wq, wkv, qg, kg, *tables]
        in_specs = [row_spec, vec_d, mod_spec, mod_spec,
                    pl.BlockSpec(wq.shape, lambda b, i: (0, 0)),
                    pl.BlockSpec(wkv.shape, lambda b, i: (0, 0)),
                    vec_l, vec_l, tab_spec, tab_spec, tab_spec]
        out_specs = [pl.BlockSpec((None, nh, tm, HEAD_DIM), lambda b, i: (b, 0, i, 0)), kv_spec, kv_spec]
        out_shape = [jax.ShapeDtypeStruct((nb, nh, s, HEAD_DIM), BF), kv_shape, kv_shape]
    else:
        nh = 0
        ins = [x2, g, sh, sc, wkv, kg]
        in_specs = [row_spec, vec_d, mod_spec, mod_spec,
                    pl.BlockSpec(wkv.shape, lambda b, i: (0, 0)), vec_l]
        out_specs = [kv_spec, kv_spec]
        out_shape = [kv_shape, kv_shape]
    return pl.pallas_call(
        functools.partial(_qkv_kernel, nh=nh, nkv=nkv, rope=with_q, with_q=with_q),
        grid=(nb, nt), in_specs=in_specs, out_specs=out_specs, out_shape=out_shape,
        compiler_params=_cparams("arbitrary", "arbitrary"),
    )(*ins)


def _attn_kernel(sink_ref, q_ref, kp_ref, kc_ref, kn_ref, vp_ref, vc_ref, vn_ref,
                 kx_ref, vx_ref, o_ref, bp_ref, bn_ref, kcat_ref, vcat_ref, *, nkv, nblk):
    i = pl.program_id(1)
    rows = GROUP * ATT_BLOCK
    w = ATT_BLOCK
    for h in range(nkv):
        for part, (k_ref, v_ref) in enumerate(((kp_ref, vp_ref), (kc_ref, vc_ref), (kn_ref, vn_ref))):
            kcat_ref[h, part * w:(part + 1) * w] = k_ref[h]
            vcat_ref[h, part * w:(part + 1) * w] = v_ref[h]
        kcat_ref[h, 3 * w:] = kx_ref[h]
        vcat_ref[h, 3 * w:] = vx_ref[h]
    qi = lax.broadcasted_iota(I32, (rows, ATT_BLOCK), 0) % ATT_BLOCK
    kj = lax.broadcasted_iota(I32, (rows, ATT_BLOCK), 1)
    prev_ok = (kj >= qi) & (i > 0)
    next_ok = (kj <= qi) & (i < nblk - 1)
    bp_ref[...] = jnp.where(prev_ok, 0.0, NEG_INF)
    bn_ref[...] = jnp.where(next_ok, 0.0, NEG_INF)
    def scores(h):
        q = q_ref[GROUP * h:GROUP * (h + 1)].reshape(rows, HEAD_DIM)
        return _dot_nt(q, kcat_ref[h])

    ahead = scores(0)
    for h in range(nkv):
        raw = ahead
        if h + 1 < nkv:
            ahead = scores(h + 1)
        s = jnp.concatenate([raw[:, :w] + bp_ref[...], raw[:, w:2 * w], raw[:, 2 * w:3 * w] + bn_ref[...],
                             raw[:, 3 * w:]], axis=1)
        sink = jnp.concatenate(
            [jnp.full((ATT_BLOCK, 1), sink_ref[GROUP * h + g] * LOG2E, F32) for g in range(GROUP)], axis=0)
        m = jnp.maximum(jnp.max(s, axis=-1, keepdims=True), sink)
        p = jnp.exp2(s - m)
        den = jnp.sum(p, axis=-1, keepdims=True) + jnp.exp2(sink - m)
        o = (_dot(p.astype(BF), vcat_ref[h]) / den).astype(BF)
        for g in range(GROUP):
            hh = GROUP * h + g
            o_ref[:, hh * HEAD_DIM:(hh + 1) * HEAD_DIM] = o[g * ATT_BLOCK:(g + 1) * ATT_BLOCK]


def _attention(q, k, v, kx, vx, sink):
    nb, nh, s, _ = q.shape
    nkv = k.shape[1]
    lc = kx.shape[2]
    nblk = s // ATT_BLOCK
    blk = lambda f: pl.BlockSpec((None, nkv, ATT_BLOCK, HEAD_DIM), f)
    prev = lambda b, i: (b, 0, jnp.maximum(i - 1, 0), 0)
    cur = lambda b, i: (b, 0, i, 0)
    nxt = lambda b, i: (b, 0, jnp.minimum(i + 1, nblk - 1), 0)
    ctx_spec = pl.BlockSpec((None, nkv, lc, HEAD_DIM), lambda b, i: (b, 0, 0, 0))
    return pl.pallas_call(
        functools.partial(_attn_kernel, nkv=nkv, nblk=nblk),
        grid=(nb, nblk),
        in_specs=[pl.BlockSpec(memory_space=pltpu.SMEM),
                  pl.BlockSpec((None, nh, ATT_BLOCK, HEAD_DIM), cur),
                  blk(prev), blk(cur), blk(nxt), blk(prev), blk(cur), blk(nxt), ctx_spec, ctx_spec],
        out_specs=pl.BlockSpec((ATT_BLOCK, nh * HEAD_DIM), lambda b, i: (b * nblk + i, 0)),
        out_shape=jax.ShapeDtypeStruct((nb * s, nh * HEAD_DIM), BF),
        scratch_shapes=[pltpu.VMEM((GROUP * ATT_BLOCK, ATT_BLOCK), F32)] * 2
        + [pltpu.VMEM((nkv, 3 * ATT_BLOCK + lc, HEAD_DIM), BF)] * 2,
        compiler_params=_cparams("arbitrary", "arbitrary"),
    )(sink, q, k, k, k, v, v, v, kx, vx)


def _mm_res_kernel(a_ref, w_ref, res_ref, gate_ref, o_ref):
    o_ref[...] = res_ref[...] + gate_ref[...] * _dot(a_ref[...], w_ref[...])


def _mm_res(a, w, res, gate, nb):
    m, kd = a.shape
    n = w.shape[1]
    tm = min(m // nb, 512)
    per_b = m // nb // tm
    return pl.pallas_call(
        _mm_res_kernel,
        grid=(m // tm,),
        in_specs=[pl.BlockSpec((tm, kd), lambda i: (i, 0)),
                  pl.BlockSpec((kd, n), lambda i: (0, 0)),
                  pl.BlockSpec((tm, n), lambda i: (i, 0)),
                  pl.BlockSpec((None, 1, n), lambda i: (i // per_b, 0, 0))],
        out_specs=pl.BlockSpec((tm, n), lambda i: (i, 0)),
        out_shape=jax.ShapeDtypeStruct((m, n), F32),
        compiler_params=_cparams("arbitrary"),
    )(a, w, res, gate)


def _pw1_kernel(x_ref, g_ref, sh_ref, sc_ref, wa_ref, wg_ref, ba_ref, bg_ref, o_ref):
    hb = _norm_mod(x_ref[...], g_ref[...], sh_ref[...], sc_ref[...]).astype(BF)
    a = _dot(hb, wa_ref[...]) + ba_ref[...]
    gt = _dot(hb, wg_ref[...]) + bg_ref[...]
    o_ref[...] = a * jax.nn.sigmoid(gt)


def _pw1_glu(x2, nb, g, sh, sc, w, bias):
    m, d = x2.shape
    tm = min(m // nb, 512)
    per_b = m // nb // tm
    tn = min(d, 1024)
    nj = d // tn
    mod_spec = pl.BlockSpec((None, 1, d), lambda j, i: (i // per_b, 0, 0))
    return pl.pallas_call(
        _pw1_kernel,
        grid=(nj, m // tm),
        in_specs=[pl.BlockSpec((tm, d), lambda j, i: (i, 0)),
                  pl.BlockSpec((1, d), lambda j, i: (0, 0)), mod_spec, mod_spec,
                  pl.BlockSpec((d, tn), lambda j, i: (0, j)),
                  pl.BlockSpec((d, tn), lambda j, i: (0, nj + j)),
                  pl.BlockSpec((1, tn), lambda j, i: (0, j)),
                  pl.BlockSpec((1, tn), lambda j, i: (0, nj + j))],
        out_specs=pl.BlockSpec((tm, tn), lambda j, i: (i, j)),
        out_shape=jax.ShapeDtypeStruct((m, d), F32),
        compiler_params=_cparams("arbitrary", "arbitrary"),
    )(x2, g, sh, sc, w, w, bias, bias)


def _conv_kernel(up_ref, uc_ref, un_ref, wdw_ref, bdw_ref, lg_ref, lb_ref, w2_ref, b2_ref,
                 res_ref, gate_ref, o_ref, ext_ref, cv_ref, *, nt):
    i = pl.program_id(1)
    tm, d = uc_ref.shape
    zero = jnp.zeros((CONV_HALO, d), F32)
    ext_ref[0:CONV_HALO, :] = jnp.where(i > 0, up_ref[...], zero)
    ext_ref[CONV_HALO:CONV_HALO + tm, :] = uc_ref[...]
    ext_ref[CONV_HALO + tm:, :] = jnp.where(i < nt - 1, un_ref[...], zero)
    off = CONV_HALO - CONV_WIDTH // 2
    rb = min(tm, CONV_ROWS)
    span = rb + 2 * CONV_HALO
    for c in range(d // LANES):
        cols = slice(c * LANES, (c + 1) * LANES)
        wcol = wdw_ref[:, cols]
        for r0 in range(0, tm, rb):
            xin = ext_ref[r0:r0 + span, cols]
            shifted = {0: xin}
            acc = jnp.broadcast_to(bdw_ref[:, cols], (rb, LANES))
            for k in range(CONV_WIDTH):
                r, a = (off + k) % SUBLANES, (off + k) // SUBLANES
                if r not in shifted:
                    shifted[r] = pltpu.roll(xin, span - r, 0)
                acc = acc + wcol[k:k + 1] * shifted[r][SUBLANES * a:SUBLANES * a + rb]
            cv_ref[r0:r0 + rb, cols] = acc
    u = cv_ref[...]
    mu = jnp.mean(u, axis=-1, keepdims=True)
    var = jnp.mean(jnp.square(u - mu), axis=-1, keepdims=True)
    y = (u - mu) * lax.rsqrt(var + EPS) * lg_ref[...] + lb_ref[...]
    y = y * jax.nn.sigmoid(y)
    out = _dot(y.astype(BF), w2_ref[...]) + b2_ref[...]
    o_ref[...] = res_ref[...] + gate_ref[...] * out


def _conv_pw2(u, nb, wdw, bdw, lg, lb, w2, b2, res, gate):
    m, d = u.shape
    s = m // nb
    tm = min(s, 256)
    nt = s // tm
    hb = tm // CONV_HALO
    nhb = s // CONV_HALO
    vec = pl.BlockSpec((1, d), lambda b, i: (0, 0))
    tile = pl.BlockSpec((tm, d), lambda b, i: (b * nt + i, 0))
    return pl.pallas_call(
        functools.partial(_conv_kernel, nt=nt),
        grid=(nb, nt),
        in_specs=[pl.BlockSpec((CONV_HALO, d), lambda b, i: (b * nhb + jnp.maximum(i * hb - 1, 0), 0)),
                  tile,
                  pl.BlockSpec((CONV_HALO, d), lambda b, i: (b * nhb + jnp.minimum((i + 1) * hb, nhb - 1), 0)),
                  pl.BlockSpec((CONV_WIDTH, d), lambda b, i: (0, 0)),
                  vec, vec, vec,
                  pl.BlockSpec((d, d), lambda b, i: (0, 0)),
                  vec, tile,
                  pl.BlockSpec((None, 1, d), lambda b, i: (b, 0, 0))],
        out_specs=tile,
        out_shape=jax.ShapeDtypeStruct((m, d), F32),
        scratch_shapes=[pltpu.VMEM((tm + 2 * CONV_HALO, d), F32), pltpu.VMEM((tm, d), F32)],
        compiler_params=_cparams("arbitrary", "arbitrary"),
    )(u, u, u, wdw, bdw, lg, lb, w2, b2, res, gate)


def _router_kernel(x_ref, g_ref, sh_ref, sc_ref, wr_ref, xp_ref, aff_ref):
    hn = _norm_mod(x_ref[...], g_ref[...], sh_ref[...], sc_ref[...])
    h_hi, h_lo = _split_bf16(hn)
    w_hi, w_lo = _split_bf16(wr_ref[...])
    logits = _dot_nt(w_hi, h_hi) + _dot_nt(w_hi, h_lo) + _dot_nt(w_lo, h_hi)
    mx = jnp.max(logits, axis=0, keepdims=True)
    ex = jnp.exp(logits - mx)
    aff_ref[...] = ex / jnp.sum(ex, axis=0, keepdims=True)
    bits = lax.bitcast_convert_type(h_hi.astype(F32), U32)
    tm = bits.shape[0]
    d2 = bits.shape[1] // 2
    word = bits[:, :d2] | (bits[:, d2:] >> 16)
    tok_rows = d2 // LANES
    for j in range(tok_rows):
        xp_ref[pl.ds(j, tm, stride=tok_rows), :] = word[:, j * LANES:(j + 1) * LANES]


def _router(x2, nb, g, sh, sc, wr_t):
    m, d = x2.shape
    s = m // nb
    ne = wr_t.shape[0]
    tm = min(s, 512)
    nt = s // tm
    tok_rows = d // 2 // LANES
    mod_spec = pl.BlockSpec((None, 1, d), lambda b, i: (b, 0, 0))
    return pl.pallas_call(
        _router_kernel,
        grid=(nb, nt),
        in_specs=[pl.BlockSpec((tm, d), lambda b, i: (b * nt + i, 0)),
                  pl.BlockSpec((1, d), lambda b, i: (0, 0)), mod_spec, mod_spec,
                  pl.BlockSpec((ne, d), lambda b, i: (0, 0))],
        out_specs=[pl.BlockSpec((tm * tok_rows, LANES), lambda b, i: (b * nt + i, 0)),
                   pl.BlockSpec((None, ne, tm), lambda b, i: (b, 0, i))],
        out_shape=[jax.ShapeDtypeStruct((m * tok_rows, LANES), U32),
                   jax.ShapeDtypeStruct((nb, ne, s), F32)],
        compiler_params=_cparams("arbitrary", "arbitrary"),
    )(x2, g, sh, sc, wr_t)


def _prefix_incl(mask_bf):
    ne, s = mask_bf.shape
    r = lax.broadcasted_iota(I32, (LANES, LANES), 0)
    c = lax.broadcasted_iota(I32, (LANES, LANES), 1)
    tri = jnp.where(r <= c, 1.0, 0.0).astype(BF)
    carry = jnp.zeros((ne, 1), F32)
    out = []
    for ch in range(s // LANES):
        inc = _dot(mask_bf[:, ch * LANES:(ch + 1) * LANES], tri) + carry
        out.append(inc)
        carry = inc[:, LANES - 1:LANES]
    return jnp.concatenate(out, axis=1)


def _topk_kernel(aff_ref, pos_ref, cnt_ref, idx_ref, gslot_ref, pos_scr, *, cap):
    aff = aff_ref[...]
    ne, s = aff.shape
    v = lax.bitcast_convert_type(aff, I32)

    def bit_step(it, prefix):
        cand = prefix | jnp.left_shift(jnp.int32(1), 30 - it)
        cnt = jnp.sum(jnp.where(v >= cand, 1.0, 0.0), axis=1, keepdims=True)
        return jnp.where(cnt >= cap, cand, prefix)

    thr = lax.fori_loop(0, 31, bit_step, jnp.zeros((ne, 1), I32))
    gt = v > thr
    eq = v == thr
    need = cap - jnp.sum(jnp.where(gt, 1.0, 0.0), axis=1, keepdims=True)
    eq_bf = jnp.where(eq, 1.0, 0.0).astype(BF)
    eq_rank = _prefix_incl(eq_bf) - eq_bf.astype(F32)
    sel = gt | (eq & (eq_rank < need))
    sel_bf = jnp.where(sel, 1.0, 0.0).astype(BF)
    incl = _prefix_incl(sel_bf)
    pos = jnp.where(sel, incl - 1.0, -1.0).astype(I32)
    pos_ref[...] = pos
    t_i = lax.broadcasted_iota(I32, (s, LANES), 0)
    j_i = lax.broadcasted_iota(I32, (s, LANES), 1)
    before = jnp.where(t_i < j_i * LANES, 1.0, 0.0).astype(BF)
    cnt_ref[...] = _dot(sel_bf, before).astype(I32)
    pos_scr[...] = pos
    t_row = lax.broadcasted_iota(I32, (SUBLANES, s), 1)
    r_row = lax.broadcasted_iota(I32, (SUBLANES, s), 0)
    tval = jnp.where(r_row == 0, t_row // 64, jnp.where(r_row == 1, t_row % 64, 0)).astype(F32)
    slot = lax.broadcasted_iota(I32, (cap, 1), 0)

    def expert_step(e, carry):
        a = aff_ref[pl.ds(e, 1), :]
        a_hi = a.astype(BF).astype(F32)
        a_mid = (a - a_hi).astype(BF).astype(F32)
        a_lo = a - a_hi - a_mid
        vals = jnp.where(r_row == 2, a_hi, jnp.where(r_row == 3, a_mid, jnp.where(r_row == 4, a_lo, tval)))
        onehot = jnp.where(pos_scr[pl.ds(e, 1), :] == slot, 1.0, 0.0).astype(BF)
        res = _dot_nt(vals.astype(BF), onehot)
        idx_ref[pl.ds(e, 1), :] = (res[0:1] * 64.0 + res[1:2]).astype(I32)
        gslot_ref[pl.ds(e, 1), :] = res[2:3] + res[3:4] + res[4:5]
        return carry

    lax.fori_loop(0, ne, expert_step, 0)


def _topk(aff_t, cap):
    nb, ne, s = aff_t.shape
    blk = pl.BlockSpec((None, ne, s), lambda b: (b, 0, 0))
    return pl.pallas_call(
        functools.partial(_topk_kernel, cap=cap),
        grid=(nb,),
        in_specs=[blk],
        out_specs=[blk,
                   pl.BlockSpec((None, ne, LANES), lambda b: (b, 0, 0)),
                   pl.BlockSpec((None, ne, cap), lambda b: (b, 0, 0)),
                   pl.BlockSpec((None, ne, cap), lambda b: (b, 0, 0))],
        out_shape=[jax.ShapeDtypeStruct((nb, ne, s), I32),
                   jax.ShapeDtypeStruct((nb, ne, LANES), I32),
                   jax.ShapeDtypeStruct((nb, ne, cap), I32),
                   jax.ShapeDtypeStruct((nb, ne, cap), F32)],
        scratch_shapes=[pltpu.VMEM((ne, s), I32)],
        compiler_params=_cparams("arbitrary"),
    )(aff_t)


GATHER_UNROLL = 8


def _token_copy(src_ref, src_tok, dst_ref, dst_tok, tok_rows, sem):
    src = pl.multiple_of(src_tok * tok_rows, tok_rows)
    dst = pl.multiple_of(dst_tok * tok_rows, tok_rows)
    return pltpu.make_async_copy(src_ref.at[pl.ds(src, tok_rows), :], dst_ref.at[pl.ds(dst, tok_rows), :], sem)


def _expert_up_kernel(cur_ref, nxt_ref, xp_ref, wg_ref, wu_ref, o_ref, raw_ref, xb_ref, sem, *, ne, n, nj):
    e = pl.program_id(0)
    j = pl.program_id(1)
    slot = e % 2
    per = n // nj
    tok_rows = raw_ref.shape[1] // n

    def issue(idx_ref, dst, lo, cnt):
        def body(k, carry):
            for i in range(GATHER_UNROLL):
                c = lo + k * GATHER_UNROLL + i
                _token_copy(xp_ref, idx_ref[0, c], raw_ref.at[dst], c, tok_rows, sem.at[dst]).start(priority=i % 2)
            return carry

        lax.fori_loop(0, cnt // GATHER_UNROLL, body, 0)

    @pl.when((e == 0) & (j == 0))
    def _():
        issue(cur_ref, 0, 0, n)

    def drain(which):
        def body(c, carry):
            _token_copy(xp_ref, 0, raw_ref.at[which], c, tok_rows, sem.at[which]).wait()
            return carry

        lax.fori_loop(0, n, body, 0, unroll=8)

    @pl.when(j == 0)
    def _():
        drain(slot)
        d2 = tok_rows * LANES
        for q in range(tok_rows):
            w = raw_ref[slot, pl.ds(q, n, stride=tok_rows), :]
            xb_ref[:, q * LANES:(q + 1) * LANES] = lax.bitcast_convert_type(
                w & jnp.uint32(0xFFFF0000), F32).astype(BF)
            xb_ref[:, d2 + q * LANES:d2 + (q + 1) * LANES] = lax.bitcast_convert_type(w << 16, F32).astype(BF)

    for i in range(per):
        c = j * per + i
        _token_copy(xp_ref, nxt_ref[0, c], raw_ref.at[1 - slot], c, tok_rows, sem.at[1 - slot]).start(priority=i % 2)

    x = xb_ref[...]
    g = _dot(x, wg_ref[...].astype(BF))
    u = _dot(x, wu_ref[...].astype(BF))
    o_ref[...] = (g * jax.nn.sigmoid(g) * u).astype(BF)

    @pl.when((e == ne - 1) & (j == nj - 1))
    def _():
        drain(1 - slot)


def _expert_up(xp, rows, w_gate, w_up, layer):
    _, ne, d, f = w_gate.shape
    n = rows.shape[2]
    tf = min(f, 256)
    nj = f // tf
    tok_rows = d // 2 // LANES
    w_spec = pl.BlockSpec((None, None, d, tf), lambda e, j: (layer, e, 0, j))
    idx_spec = lambda f_: pl.BlockSpec((None, 1, n), f_, memory_space=pltpu.SMEM)
    return pl.pallas_call(
        functools.partial(_expert_up_kernel, ne=ne, n=n, nj=nj),
        grid=(ne, nj),
        in_specs=[idx_spec(lambda e, j: (e, 0, 0)),
                  idx_spec(lambda e, j: (jnp.minimum(e + 1, ne - 1), 0, 0)),
                  pl.BlockSpec(memory_space=pl.ANY), w_spec, w_spec],
        out_specs=pl.BlockSpec((None, n, tf), lambda e, j: (e, 0, j)),
        out_shape=jax.ShapeDtypeStruct((ne, n, f), BF),
        scratch_shapes=[pltpu.VMEM((2, n * tok_rows, LANES), U32), pltpu.VMEM((n, d), BF),
                        pltpu.SemaphoreType.DMA((2,))],
        compiler_params=_cparams("arbitrary", "arbitrary"),
    )(rows, rows, xp, w_gate, w_up)


def _expert_down_kernel(h_ref, w_ref, gs_ref, o_ref):
    o_ref[...] = _dot(h_ref[...], w_ref[...].astype(BF)) * gs_ref[...]


def _expert_down(hid, w_down, gslot, layer):
    ne, n, f = hid.shape
    d = w_down.shape[3]
    tn = min(d, 512)
    return pl.pallas_call(
        _expert_down_kernel,
        grid=(ne, d // tn),
        in_specs=[pl.BlockSpec((None, n, f), lambda e, j: (e, 0, 0)),
                  pl.BlockSpec((None, None, f, tn), lambda e, j: (layer, e, 0, j)),
                  pl.BlockSpec((None, n, 1), lambda e, j: (e, 0, 0))],
        out_specs=pl.BlockSpec((None, n, tn), lambda e, j: (e, 0, j)),
        out_shape=jax.ShapeDtypeStruct((ne, n, d), F32),
        compiler_params=_cparams("arbitrary", "arbitrary"),
    )(hid, w_down, gslot)


COMBINE_TOKENS = 128
COMBINE_CHUNK = 256
COMBINE_STATIC_CHUNKS = 2
SLAB_ROWS = (128, 64, 32, 16, 8)
COMBINE_SLOTS_PER_EXPERT = COMBINE_TOKENS + SUBLANES


def _slab_copy(y_ref, src_row, buf_ref, dst_row, rows, sem):
    return pltpu.make_async_copy(y_ref.at[pl.ds(src_row, rows), :], buf_ref.at[pl.ds(dst_row, rows), :], sem)


def _combine_kernel(cnt_ref, y_ref, pos_ref, res_ref, g2_ref, o_ref, buf_ref, acc_ref, slot_ref, sem,
                    *, ne, n, cap, nbound):
    b = pl.program_id(0)
    t = pl.program_id(1)
    nt = nbound - 1
    step = b * nt + t
    last = pl.num_programs(0) * nt - 1

    @pl.when(step == 0)
    def _():
        buf_ref[...] = jnp.zeros_like(buf_ref)

    def plan(bq, tq):
        slabs, shift = [], []
        total = jnp.int32(0)
        for e in range(ne):
            base = (bq * ne + e) * nbound + tq
            c0 = cnt_ref[base]
            c1 = cnt_ref[base + 1]
            a0 = (c0 // SUBLANES) * SUBLANES
            run = jnp.where(c1 > c0, (c1 + SUBLANES - 1) // SUBLANES * SUBLANES - a0, 0)
            row0 = e * n + bq * cap + a0
            done = jnp.int32(0)
            for rows in SLAB_ROWS:
                take = (run & rows) != 0
                slabs.append((take, rows, pl.multiple_of(row0 + done, SUBLANES),
                              pl.multiple_of(total + done, SUBLANES)))
                done = done + jnp.where(take, rows, 0)
            shift.append(total - a0)
            total = total + run
        return slabs, shift, total

    def start_all(slabs, half):
        for take, rows, src, dst in slabs:
            @pl.when(take)
            def _(rows=rows, src=src, dst=dst):
                _slab_copy(y_ref, src, buf_ref.at[half], dst, rows, sem.at[half]).start()

    half = step % 2
    slabs, shift, total = plan(b, t)

    @pl.when(step == 0)
    def _():
        start_all(slabs, half)

    nxt = jnp.minimum(step + 1, last)
    nxt_slabs, _, _ = plan(nxt // nt, nxt % nt)

    @pl.when(step < last)
    def _():
        start_all(nxt_slabs, 1 - half)

    for take, rows, _, _ in slabs:
        @pl.when(take)
        def _(rows=rows):
            _slab_copy(y_ref, 0, buf_ref.at[half], 0, rows, sem.at[half]).wait()

    pos = pos_ref[...]
    for e in range(ne):
        pe = pos[:, e:e + 1]
        where_e = jnp.where(pe >= 0, pe + shift[e], -1)
        slot_ref[e] = jnp.broadcast_to(where_e, (COMBINE_TOKENS, LANES))

    def expand(start):
        parts = []
        for q in range(COMBINE_CHUNK // LANES):
            lane = lax.broadcasted_iota(I32, (COMBINE_TOKENS, LANES), 1) + (start + q * LANES)
            hot = jnp.zeros((COMBINE_TOKENS, LANES), F32)
            for e in range(ne):
                hot = jnp.where(slot_ref[e] == lane, 1.0, hot)
            parts.append(hot.astype(BF))
        onehot = jnp.concatenate(parts, axis=1)
        live = lax.broadcasted_iota(I32, (COMBINE_CHUNK, 1), 0) + start < total
        rows = jnp.where(live, buf_ref[half, pl.ds(start, COMBINE_CHUNK), :], 0.0)
        r_hi, r_lo = _split_bf16(rows)
        return _dot(onehot, r_hi) + _dot(onehot, r_lo)

    acc_ref[...] = sum(expand(k * COMBINE_CHUNK) for k in range(COMBINE_STATIC_CHUNKS))

    def chunk(ci, carry):
        acc_ref[...] += expand(pl.multiple_of(ci * COMBINE_CHUNK, COMBINE_CHUNK))
        return carry

    lax.fori_loop(COMBINE_STATIC_CHUNKS, (total + COMBINE_CHUNK - 1) // COMBINE_CHUNK, chunk, 0)
    o_ref[...] = res_ref[...] + g2_ref[...] * acc_ref[...]


def _combine(cnt, y, pos_t, res, g2, nb, cap):
    m, d = res.shape
    s = m // nb
    ne = pos_t.shape[1]
    n = nb * cap
    nt = s // COMBINE_TOKENS
    nbound = nt + 1
    tok = lambda b, t, c: (b * nt + t, 0)
    grid_spec = pltpu.PrefetchScalarGridSpec(
        num_scalar_prefetch=1,
        grid=(nb, nt),
        in_specs=[pl.BlockSpec(memory_space=pl.ANY),
                  pl.BlockSpec((COMBINE_TOKENS, ne), tok),
                  pl.BlockSpec((COMBINE_TOKENS, d), tok),
                  pl.BlockSpec((None, 1, d), lambda b, t, c: (b, 0, 0))],
        out_specs=pl.BlockSpec((COMBINE_TOKENS, d), tok),
        scratch_shapes=[pltpu.VMEM((2, pl.cdiv(ne * COMBINE_SLOTS_PER_EXPERT, COMBINE_CHUNK) * COMBINE_CHUNK, d), F32),
                        pltpu.VMEM((COMBINE_TOKENS, d), F32),
                        pltpu.VMEM((ne, COMBINE_TOKENS, LANES), I32),
                        pltpu.SemaphoreType.DMA((2,))],
    )
    return pl.pallas_call(
        functools.partial(_combine_kernel, ne=ne, n=n, cap=cap, nbound=nbound),
        grid_spec=grid_spec,
        out_shape=jax.ShapeDtypeStruct((m, d), F32),
        compiler_params=_cparams("arbitrary", "arbitrary"),
    )(cnt[:, :, :nbound].reshape(-1), y, pos_t, res, g2)


def _moe(h2, nb, g, sh, sc, gate2, w_router, w_gate, w_up, w_down, layer):
    m, d = h2.shape
    s = m // nb
    ne = w_router.shape[1]
    cap = EC_FACTOR * s // ne
    xp, aff_t = _router(h2, nb, g, sh, sc, w_router.T)
    pos, cnt, idx, gslot = _topk(aff_t, cap)
    rows = idx + (jnp.arange(nb, dtype=I32) * s)[:, None, None]
    rows = jnp.swapaxes(rows, 0, 1).reshape(ne, 1, nb * cap)
    gslot = jnp.swapaxes(gslot, 0, 1).reshape(ne, nb * cap, 1)
    hid = _expert_up(xp, rows, w_gate, w_up, layer)
    y = _expert_down(hid, w_down, gslot, layer).reshape(ne * nb * cap, d)
    pos_t = jnp.swapaxes(pos, 1, 2).reshape(m, ne)
    return _combine(cnt, y, pos_t, h2, gate2, nb, cap)


def _rope_tables(s):
    rows = s // GRID_W
    r, col = jnp.meshgrid(jnp.arange(rows, dtype=F32), jnp.arange(GRID_W, dtype=F32), indexing="ij")
    r, col = r.reshape(-1), col.reshape(-1)
    n_pairs = HEAD_DIM // 4
    inv = ROPE_BASE ** (-jnp.arange(n_pairs, dtype=F32) / n_pairs)
    ang_r = r[:, None] * inv[None, :]
    ang_c = col[:, None] * inv[None, :]
    ang = jnp.concatenate([ang_r, ang_r, ang_c, ang_c], axis=-1)
    cos, sin = jnp.cos(ang), jnp.sin(ang)
    first = (jnp.arange(HEAD_DIM) % (HEAD_DIM // 2)) < HEAD_DIM // 4
    sin_a = jnp.where(first, -sin, 0.0)
    sin_b = jnp.where(first, 0.0, sin)
    two = lambda a: jnp.concatenate([a, a], axis=-1)
    return two(cos), two(sin_a), two(sin_b)


def kernel(x, c, ctx, c_ctx, ada_w, ada_b, norm1_g, norm2_g, attn_w_q, attn_w_kv, attn_w_o, attn_q_gain, attn_k_gain, attn_sink, conv_w_pw1, conv_b_pw1, conv_w_dw, conv_b_dw, conv_ln_g, conv_ln_b, conv_w_pw2, conv_b_pw2, moe_router, moe_w_gate, moe_w_up, moe_w_down):
    nb, s, d = x.shape
    lc = ctx.shape[1]
    depth = ada_w.shape[0]
    assert depth == 2, "layer 0 attention, layer 1 convolution; the context stream is never updated"
    m = nb * s

    cin = jnp.concatenate([c, c_ctx[None, :], jnp.zeros((8 - nb - 1, d), F32)], axis=0)
    mod = _ada(cin, ada_w, ada_b)

    def mods(layer, row0, nrow):
        return [mod[layer, row0:row0 + nrow, k * d:(k + 1) * d].reshape(nrow, 1, d) for k in range(6)]

    h = x.reshape(m, d)
    two = lambda a: jnp.concatenate([a, a], axis=-1).reshape(1, LANES)

    sh1, sc1, g1, sh2, sc2, g2 = mods(0, 0, nb)
    sh1c, sc1c = mods(0, nb, 1)[:2]
    n1 = norm1_g[0].reshape(1, d)
    wq = attn_w_q[0].astype(BF)
    wkv = attn_w_kv[0].astype(BF)
    wo = attn_w_o[0].astype(BF)
    q_gain = two(attn_q_gain[0]) * (HEAD_DIM ** -0.5 * LOG2E)
    k_gain = two(attn_k_gain[0])
    q, k, v = _qkv_proj(h, nb, n1, sh1, sc1, wq, wkv, q_gain, k_gain, _rope_tables(s))
    kx, vx = _qkv_proj(ctx.reshape(nb * lc, d), nb, n1, sh1c, sc1c, None, wkv, None, k_gain, None)
    att = _attention(q, k, v, kx, vx, attn_sink[0])
    h = _mm_res(att, wo, h, g1, nb)
    h = _moe(h, nb, norm2_g[0].reshape(1, d), sh2, sc2, g2,
             moe_router[0], moe_w_gate, moe_w_up, moe_w_down, 0)

    sh1, sc1, g1, sh2, sc2, g2 = mods(1, 0, nb)
    u = _pw1_glu(h, nb, norm1_g[1].reshape(1, d), sh1, sc1,
                 conv_w_pw1[0].astype(BF), conv_b_pw1[0].reshape(1, 2 * d))
    h = _conv_pw2(u, nb, conv_w_dw[0], conv_b_dw[0].reshape(1, d), conv_ln_g[0].reshape(1, d),
                  conv_ln_b[0].reshape(1, d), conv_w_pw2[0].astype(BF), conv_b_pw2[0].reshape(1, d), h, g1)
    h = _moe(h, nb, norm2_g[1].reshape(1, d), sh2, sc2, g2,
             moe_router[1], moe_w_gate, moe_w_up, moe_w_down, 1)
    return h.reshape(nb, s, d)
```

```python
import functools

import jax
import jax.numpy as jnp
from jax import lax
from jax.experimental import pallas as pl
from jax.experimental.pallas import tpu as pltpu

F32 = jnp.float32
BF = jnp.bfloat16
I32 = jnp.int32
U32 = jnp.uint32

HEAD_DIM = 64
GROUP = 4
GRID_W = 64
ROPE_BASE = 10000.0
ATT_BLOCK = 128
CONV_WIDTH = 31
CONV_HALO = 16
CONV_ROWS = 64
SUBLANES = 8
N_EXPERTS = 16
EC_FACTOR = 2
EPS = 1e-6
NEG_INF = -1e30
LOG2E = 1.4426950408889634
LANES = 128
V7X_VMEM_LIMIT = 56 * 1024 * 1024


def _cparams(*sem):
    return pltpu.CompilerParams(dimension_semantics=sem, vmem_limit_bytes=V7X_VMEM_LIMIT)


def _dot(a, b):
    return jnp.dot(a, b, preferred_element_type=F32)


def _dot_nt(a, b):
    return lax.dot_general(a, b, (((1,), (1,)), ((), ())), preferred_element_type=F32)


def _split_bf16(x):
    hi = x.astype(BF)
    lo = (x - hi.astype(F32)).astype(BF)
    return hi, lo


def _norm_mod(x, g, sh, sc):
    ms = jnp.mean(x * x, axis=-1, keepdims=True)
    return (x * lax.rsqrt(ms + EPS) * g) * (1.0 + sc) + sh


def _ada_kernel(c_ref, w_ref, b_ref, o_ref):
    c = c_ref[...]
    s = c * jax.nn.sigmoid(c)
    o_ref[...] = _dot(s.astype(BF), w_ref[...].astype(BF)) + b_ref[...]


def _ada(cin, ada_w, ada_b):
    depth, d, n = ada_w.shape
    rows = cin.shape[0]
    tn = min(n, 1024)
    return pl.pallas_call(
        _ada_kernel,
        grid=(depth, n // tn),
        in_specs=[pl.BlockSpec((rows, d), lambda l, j: (0, 0)),
                  pl.BlockSpec((None, d, tn), lambda l, j: (l, 0, j)),
                  pl.BlockSpec((None, 1, tn), lambda l, j: (l, 0, j))],
        out_specs=pl.BlockSpec((None, rows, tn), lambda l, j: (l, 0, j)),
        out_shape=jax.ShapeDtypeStruct((depth, rows, n), F32),
        compiler_params=_cparams("arbitrary", "arbitrary"),
    )(cin, ada_w, ada_b.reshape(depth, 1, n))


def _head_norm(y, gain):
    w = y.shape[1]
    r = lax.broadcasted_iota(I32, (w, w), 0) // HEAD_DIM
    c = lax.broadcasted_iota(I32, (w, w), 1) // HEAD_DIM
    bd = jnp.where(r == c, 1.0 / HEAD_DIM, 0.0).astype(BF)
    hi, lo = _split_bf16(y * y)
    ms = _dot(hi, bd) + _dot(lo, bd)
    return y * lax.rsqrt(ms + EPS) * jnp.concatenate([gain] * (w // LANES), axis=1)


def _rope(y, cos, sin_a, sin_b):
    return (y * cos + pltpu.roll(y, LANES - HEAD_DIM // 4, 1) * sin_a
            + pltpu.roll(y, HEAD_DIM // 4, 1) * sin_b)


def _qkv_kernel(*refs, nh, nkv, rope, with_q):
    if with_q:
        (x_ref, g_ref, sh_ref, sc_ref, wq_ref, wkv_ref, qg_ref, kg_ref,
         cos_ref, sa_ref, sb_ref, q_ref, k_ref, v_ref) = refs
    else:
        (x_ref, g_ref, sh_ref, sc_ref, wkv_ref, kg_ref, k_ref, v_ref) = refs
    hb = _norm_mod(x_ref[...], g_ref[...], sh_ref[...], sc_ref[...]).astype(BF)
    if rope:
        cos, sin_a, sin_b = cos_ref[...], sa_ref[...], sb_ref[...]
    kvd = nkv * HEAD_DIM
    kv = _dot(hb, wkv_ref[...])
    def finish(y_wide, gain, out_ref, head0):
        yn = _head_norm(y_wide, gain)
        for part in range(y_wide.shape[1] // LANES):
            y = yn[:, part * LANES:(part + 1) * LANES]
            if rope:
                y = _rope(y, cos, sin_a, sin_b)
            yb = y.astype(BF)
            out_ref[head0 + 2 * part] = yb[:, :HEAD_DIM]
            out_ref[head0 + 2 * part + 1] = yb[:, HEAD_DIM:]

    cw = min(2 * LANES, kvd)
    per = cw // HEAD_DIM
    for c in range(kvd // cw):
        finish(kv[:, c * cw:(c + 1) * cw], kg_ref[...], k_ref, per * c)
    for h in range(nkv):
        v_ref[h] = kv[:, kvd + h * HEAD_DIM:kvd + (h + 1) * HEAD_DIM].astype(BF)
    if with_q:
        nq = nh * HEAD_DIM // cw
        ahead = _dot(hb, wq_ref[:, :cw])
        for c in range(nq):
            qc = ahead
            if c + 1 < nq:
                ahead = _dot(hb, wq_ref[:, (c + 1) * cw:(c + 2) * cw])
            finish(qc, qg_ref[...], q_ref, per * c)


def _qkv_proj(x2, nb, g, sh, sc, wq, wkv, qg, kg, tables):
    m, d = x2.shape
    s = m // nb
    nkv = wkv.shape[1] // (2 * HEAD_DIM)
    with_q = wq is not None
    tm = min(s, 256)
    nt = s // tm
    shared = sh.shape[0] == 1
    mod_map = (lambda b, i: (0, 0, 0)) if shared else (lambda b, i: (b, 0, 0))
    row_spec = pl.BlockSpec((tm, d), lambda b, i: (b * nt + i, 0))
    vec_d = pl.BlockSpec((1, d), lambda b, i: (0, 0))
    mod_spec = pl.BlockSpec((None, 1, d), mod_map)
    vec_l = pl.BlockSpec((1, LANES), lambda b, i: (0, 0))
    tab_spec = pl.BlockSpec((tm, LANES), lambda b, i: (i, 0))
    kv_spec = pl.BlockSpec((None, nkv, tm, HEAD_DIM), lambda b, i: (b, 0, i, 0))
    kv_shape = jax.ShapeDtypeStruct((nb, nkv, s, HEAD_DIM), BF)
    if with_q:
        nh = wq.shape[1] // HEAD_DIM
        ins = [x2, g, sh, sc, wq, wkv, qg, kg, *tables]
        in_specs = [row_spec, vec_d, mod_spec, mod_spec,
                    pl.BlockSpec(wq.shape, lambda b, i: (0, 0)),
                    pl.BlockSpec(wkv.shape, lambda b, i: (0, 0)),
                    vec_l, vec_l, tab_spec, tab_spec, tab_spec]
        out_specs = [pl.BlockSpec((None, nh, tm, HEAD_DIM), lambda b, i: (b, 0, i, 0)), kv_spec, kv_spec]
        out_shape = [jax.ShapeDtypeStruct((nb, nh, s, HEAD_DIM), BF), kv_shape, kv_shape]
    else:
        nh = 0
        ins = [x2, g, sh, sc, wkv, kg]
        in_specs = [row_spec, vec_d, mod_spec, mod_spec,
                    pl.BlockSpec(wkv.shape, lambda b, i: (0, 0)), vec_l]
        out_specs = [kv_spec, kv_spec]
        out_shape = [kv_shape, kv_shape]
    return pl.pallas_call(
        functools.partial(_qkv_kernel, nh=nh, nkv=nkv, rope=with_q, with_q=with_q),
        grid=(nb, nt), in_specs=in_specs, out_specs=out_specs, out_shape=out_shape,
        compiler_params=_cparams("arbitrary", "arbitrary"),
    )(*ins)


def _attn_kernel(sink_ref, q_ref, kp_ref, kc_ref, kn_ref, vp_ref, vc_ref, vn_ref,
                 kx_ref, vx_ref, o_ref, bp_ref, bn_ref, kcat_ref, vcat_ref, *, nkv, nblk):
    i = pl.program_id(1)
    rows = GROUP * ATT_BLOCK
    w = ATT_BLOCK
    for h in range(nkv):
        for part, (k_ref, v_ref) in enumerate(((kp_ref, vp_ref), (kc_ref, vc_ref), (kn_ref, vn_ref))):
            kcat_ref[h, part * w:(part + 1) * w] = k_ref[h]
            vcat_ref[h, part * w:(part + 1) * w] = v_ref[h]
        kcat_ref[h, 3 * w:] = kx_ref[h]
        vcat_ref[h, 3 * w:] = vx_ref[h]
    qi = lax.broadcasted_iota(I32, (rows, ATT_BLOCK), 0) % ATT_BLOCK
    kj = lax.broadcasted_iota(I32, (rows, ATT_BLOCK), 1)
    prev_ok = (kj >= qi) & (i > 0)
    next_ok = (kj <= qi) & (i < nblk - 1)
    bp_ref[...] = jnp.where(prev_ok, 0.0, NEG_INF)
    bn_ref[...] = jnp.where(next_ok, 0.0, NEG_INF)
    def scores(h):
        q = q_ref[GROUP * h:GROUP * (h + 1)].reshape(rows, HEAD_DIM)
        return _dot_nt(q, kcat_ref[h])

    ahead = scores(0)
    for h in range(nkv):
        raw = ahead
        if h + 1 < nkv:
            ahead = scores(h + 1)
        s = jnp.concatenate([raw[:, :w] + bp_ref[...], raw[:, w:2 * w], raw[:, 2 * w:3 * w] + bn_ref[...],
                             raw[:, 3 * w:]], axis=1)
        sink = jnp.concatenate(
            [jnp.full((ATT_BLOCK, 1), sink_ref[GROUP * h + g] * LOG2E, F32) for g in range(GROUP)], axis=0)
        m = jnp.maximum(jnp.max(s, axis=-1, keepdims=True), sink)
        p = jnp.exp2(s - m)
        den = jnp.sum(p, axis=-1, keepdims=True) + jnp.exp2(sink - m)
        o = (_dot(p.astype(BF), vcat_ref[h]) / den).astype(BF)
        for g in range(GROUP):
            hh = GROUP * h + g
            o_ref[:, hh * HEAD_DIM:(hh + 1) * HEAD_DIM] = o[g * ATT_BLOCK:(g + 1) * ATT_BLOCK]


def _attention(q, k, v, kx, vx, sink):
    nb, nh, s, _ = q.shape
    nkv = k.shape[1]
    lc = kx.shape[2]
    nblk = s // ATT_BLOCK
    blk = lambda f: pl.BlockSpec((None, nkv, ATT_BLOCK, HEAD_DIM), f)
    prev = lambda b, i: (b, 0, jnp.maximum(i - 1, 0), 0)
    cur = lambda b, i: (b, 0, i, 0)
    nxt = lambda b, i: (b, 0, jnp.minimum(i + 1, nblk - 1), 0)
    ctx_spec = pl.BlockSpec((None, nkv, lc, HEAD_DIM), lambda b, i: (b, 0, 0, 0))
    return pl.pallas_call(
        functools.partial(_attn_kernel, nkv=nkv, nblk=nblk),
        grid=(nb, nblk),
        in_specs=[pl.BlockSpec(memory_space=pltpu.SMEM),
                  pl.BlockSpec((None, nh, ATT_BLOCK, HEAD_DIM), cur),
                  blk(prev), blk(cur), blk(nxt), blk(prev), blk(cur), blk(nxt), ctx_spec, ctx_spec],
        out_specs=pl.BlockSpec((ATT_BLOCK, nh * HEAD_DIM), lambda b, i: (b * nblk + i, 0)),
        out_shape=jax.ShapeDtypeStruct((nb * s, nh * HEAD_DIM), BF),
        scratch_shapes=[pltpu.VMEM((GROUP * ATT_BLOCK, ATT_BLOCK), F32)] * 2
        + [pltpu.VMEM((nkv, 3 * ATT_BLOCK + lc, HEAD_DIM), BF)] * 2,
        compiler_params=_cparams("arbitrary", "arbitrary"),
    )(sink, q, k, k, k, v, v, v, kx, vx)


def _mm_res_kernel(a_ref, w_ref, res_ref, gate_ref, o_ref):
    o_ref[...] = res_ref[...] + gate_ref[...] * _dot(a_ref[...], w_ref[...])


def _mm_res(a, w, res, gate, nb):
    m, kd = a.shape
    n = w.shape[1]
    tm = min(m // nb, 512)
    per_b = m // nb // tm
    return pl.pallas_call(
        _mm_res_kernel,
        grid=(m // tm,),
        in_specs=[pl.BlockSpec((tm, kd), lambda i: (i, 0)),
                  pl.BlockSpec((kd, n), lambda i: (0, 0)),
                  pl.BlockSpec((tm, n), lambda i: (i, 0)),
                  pl.BlockSpec((None, 1, n), lambda i: (i // per_b, 0, 0))],
        out_specs=pl.BlockSpec((tm, n), lambda i: (i, 0)),
        out_shape=jax.ShapeDtypeStruct((m, n), F32),
        compiler_params=_cparams("arbitrary"),
    )(a, w, res, gate)


def _pw1_kernel(x_ref, g_ref, sh_ref, sc_ref, wa_ref, wg_ref, ba_ref, bg_ref, o_ref):
    tm = x_ref.shape[0]
    for r0 in range(0, tm, tm // 2):
        rows = slice(r0, r0 + tm // 2)
        hb = _norm_mod(x_ref[rows], g_ref[...], sh_ref[...], sc_ref[...]).astype(BF)
        a = _dot(hb, wa_ref[...]) + ba_ref[...]
        gt = _dot(hb, wg_ref[...]) + bg_ref[...]
        o_ref[rows] = a * jax.nn.sigmoid(gt)


def _pw1_glu(x2, nb, g, sh, sc, w, bias):
    m, d = x2.shape
    tm = min(m // nb, 512)
    per_b = m // nb // tm
    tn = min(d, 1024)
    nj = d // tn
    mod_spec = pl.BlockSpec((None, 1, d), lambda j, i: (i // per_b, 0, 0))
    return pl.pallas_call(
        _pw1_kernel,
        grid=(nj, m // tm),
        in_specs=[pl.BlockSpec((tm, d), lambda j, i: (i, 0)),
                  pl.BlockSpec((1, d), lambda j, i: (0, 0)), mod_spec, mod_spec,
                  pl.BlockSpec((d, tn), lambda j, i: (0, j)),
                  pl.BlockSpec((d, tn), lambda j, i: (0, nj + j)),
                  pl.BlockSpec((1, tn), lambda j, i: (0, j)),
                  pl.BlockSpec((1, tn), lambda j, i: (0, nj + j))],
        out_specs=pl.BlockSpec((tm, tn), lambda j, i: (i, j)),
        out_shape=jax.ShapeDtypeStruct((m, d), F32),
        compiler_params=_cparams("arbitrary", "arbitrary"),
    )(x2, g, sh, sc, w, w, bias, bias)


def _conv_kernel(up_ref, uc_ref, un_ref, wdw_ref, bdw_ref, lg_ref, lb_ref, w2_ref, b2_ref,
                 res_ref, gate_ref, o_ref, ext_ref, cv_ref, *, nt):
    i = pl.program_id(1)
    tm, d = uc_ref.shape
    zero = jnp.zeros((CONV_HALO, d), F32)
    ext_ref[0:CONV_HALO, :] = jnp.where(i > 0, up_ref[...], zero)
    ext_ref[CONV_HALO:CONV_HALO + tm, :] = uc_ref[...]
    ext_ref[CONV_HALO + tm:, :] = jnp.where(i < nt - 1, un_ref[...], zero)
    off = CONV_HALO - CONV_WIDTH // 2
    rb = min(tm, CONV_ROWS)
    span = rb + 2 * CONV_HALO
    for c in range(d // LANES):
        cols = slice(c * LANES, (c + 1) * LANES)
        wcol = wdw_ref[:, cols]
        for r0 in range(0, tm, rb):
            xin = ext_ref[r0:r0 + span, cols]
            shifted = {0: xin}
            acc = jnp.broadcast_to(bdw_ref[:, cols], (rb, LANES))
            for k in range(CONV_WIDTH):
                r, a = (off + k) % SUBLANES, (off + k) // SUBLANES
                if r not in shifted:
                    shifted[r] = pltpu.roll(xin, span - r, 0)
                acc = acc + wcol[k:k + 1] * shifted[r][SUBLANES * a:SUBLANES * a + rb]
            cv_ref[r0:r0 + rb, cols] = acc
    u = cv_ref[...]
    mu = jnp.mean(u, axis=-1, keepdims=True)
    var = jnp.mean(jnp.square(u - mu), axis=-1, keepdims=True)
    y = (u - mu) * lax.rsqrt(var + EPS) * lg_ref[...] + lb_ref[...]
    y = y * jax.nn.sigmoid(y)
    out = _dot(y.astype(BF), w2_ref[...]) + b2_ref[...]
    o_ref[...] = res_ref[...] + gate_ref[...] * out


def _conv_pw2(u, nb, wdw, bdw, lg, lb, w2, b2, res, gate):
    m, d = u.shape
    s = m // nb
    tm = min(s, 256)
    nt = s // tm
    hb = tm // CONV_HALO
    nhb = s // CONV_HALO
    vec = pl.BlockSpec((1, d), lambda b, i: (0, 0))
    tile = pl.BlockSpec((tm, d), lambda b, i: (b * nt + i, 0))
    return pl.pallas_call(
        functools.partial(_conv_kernel, nt=nt),
        grid=(nb, nt),
        in_specs=[pl.BlockSpec((CONV_HALO, d), lambda b, i: (b * nhb + jnp.maximum(i * hb - 1, 0), 0)),
                  tile,
                  pl.BlockSpec((CONV_HALO, d), lambda b, i: (b * nhb + jnp.minimum((i + 1) * hb, nhb - 1), 0)),
                  pl.BlockSpec((CONV_WIDTH, d), lambda b, i: (0, 0)),
                  vec, vec, vec,
                  pl.BlockSpec((d, d), lambda b, i: (0, 0)),
                  vec, tile,
                  pl.BlockSpec((None, 1, d), lambda b, i: (b, 0, 0))],
        out_specs=tile,
        out_shape=jax.ShapeDtypeStruct((m, d), F32),
        scratch_shapes=[pltpu.VMEM((tm + 2 * CONV_HALO, d), F32), pltpu.VMEM((tm, d), F32)],
        compiler_params=_cparams("arbitrary", "arbitrary"),
    )(u, u, u, wdw, bdw, lg, lb, w2, b2, res, gate)


def _router_kernel(x_ref, g_ref, sh_ref, sc_ref, wr_ref, xp_ref, aff_ref):
    hn = _norm_mod(x_ref[...], g_ref[...], sh_ref[...], sc_ref[...])
    h_hi, h_lo = _split_bf16(hn)
    w_hi, w_lo = _split_bf16(wr_ref[...])
    logits = _dot_nt(w_hi, h_hi) + _dot_nt(w_hi, h_lo) + _dot_nt(w_lo, h_hi)
    mx = jnp.max(logits, axis=0, keepdims=True)
    ex = jnp.exp(logits - mx)
    aff_ref[...] = ex / jnp.sum(ex, axis=0, keepdims=True)
    bits = lax.bitcast_convert_type(h_hi.astype(F32), U32)
    tm = bits.shape[0]
    d2 = bits.shape[1] // 2
    word = bits[:, :d2] | (bits[:, d2:] >> 16)
    tok_rows = d2 // LANES
    for j in range(tok_rows):
        xp_ref[pl.ds(j, tm, stride=tok_rows), :] = word[:, j * LANES:(j + 1) * LANES]


def _router(x2, nb, g, sh, sc, wr_t):
    m, d = x2.shape
    s = m // nb
    ne = wr_t.shape[0]
    tm = min(s, 512)
    nt = s // tm
    tok_rows = d // 2 // LANES
    mod_spec = pl.BlockSpec((None, 1, d), lambda b, i: (b, 0, 0))
    return pl.pallas_call(
        _router_kernel,
        grid=(nb, nt),
        in_specs=[pl.BlockSpec((tm, d), lambda b, i: (b * nt + i, 0)),
                  pl.BlockSpec((1, d), lambda b, i: (0, 0)), mod_spec, mod_spec,
                  pl.BlockSpec((ne, d), lambda b, i: (0, 0))],
        out_specs=[pl.BlockSpec((tm * tok_rows, LANES), lambda b, i: (b * nt + i, 0)),
                   pl.BlockSpec((None, ne, tm), lambda b, i: (b, 0, i))],
        out_shape=[jax.ShapeDtypeStruct((m * tok_rows, LANES), U32),
                   jax.ShapeDtypeStruct((nb, ne, s), F32)],
        compiler_params=_cparams("arbitrary", "arbitrary"),
    )(x2, g, sh, sc, wr_t)


IDX_SPLIT = 64


def _prefix_incl(mask_bf):
    ne, s = mask_bf.shape
    r = lax.broadcasted_iota(I32, (LANES, LANES), 0)
    c = lax.broadcasted_iota(I32, (LANES, LANES), 1)
    tri = jnp.where(r <= c, 1.0, 0.0).astype(BF)
    carry = jnp.zeros((ne, 1), F32)
    out = []
    for ch in range(s // LANES):
        inc = _dot(mask_bf[:, ch * LANES:(ch + 1) * LANES], tri) + carry
        out.append(inc)
        carry = inc[:, LANES - 1:LANES]
    return jnp.concatenate(out, axis=1)


def _topk_kernel(aff_ref, pos_ref, cnt_ref, idx_ref, gslot_ref, pos_scr, *, cap):
    aff = aff_ref[...]
    ne, s = aff.shape
    v = lax.bitcast_convert_type(aff, I32)

    def bit_step(it, prefix):
        cand = prefix | jnp.left_shift(jnp.int32(1), 30 - it)
        cnt = jnp.sum(jnp.where(v >= cand, 1.0, 0.0), axis=1, keepdims=True)
        return jnp.where(cnt >= cap, cand, prefix)

    thr = lax.fori_loop(0, 31, bit_step, jnp.zeros((ne, 1), I32))
    gt = v > thr
    eq = v == thr
    need = cap - jnp.sum(jnp.where(gt, 1.0, 0.0), axis=1, keepdims=True)
    eq_bf = jnp.where(eq, 1.0, 0.0).astype(BF)
    eq_rank = _prefix_incl(eq_bf) - eq_bf.astype(F32)
    sel = gt | (eq & (eq_rank < need))
    sel_bf = jnp.where(sel, 1.0, 0.0).astype(BF)
    incl = _prefix_incl(sel_bf)
    pos = jnp.where(sel, incl - 1.0, -1.0).astype(I32)
    pos_ref[...] = pos
    t_i = lax.broadcasted_iota(I32, (s, LANES), 0)
    j_i = lax.broadcasted_iota(I32, (s, LANES), 1)
    before = jnp.where(t_i < j_i * LANES, 1.0, 0.0).astype(BF)
    cnt_ref[...] = _dot(sel_bf, before).astype(I32)
    pos_scr[...] = pos
    t_row = lax.broadcasted_iota(I32, (SUBLANES, s), 1)
    r_row = lax.broadcasted_iota(I32, (SUBLANES, s), 0)
    tval = jnp.where(r_row == 0, t_row // IDX_SPLIT, jnp.where(r_row == 1, t_row % IDX_SPLIT, 0)).astype(F32)
    slot = lax.broadcasted_iota(I32, (cap, 1), 0)

    def expert_step(e, carry):
        a = aff_ref[pl.ds(e, 1), :]
        a_hi = a.astype(BF).astype(F32)
        a_mid = (a - a_hi).astype(BF).astype(F32)
        a_lo = a - a_hi - a_mid
        vals = jnp.where(r_row == 2, a_hi, jnp.where(r_row == 3, a_mid, jnp.where(r_row == 4, a_lo, tval)))
        onehot = jnp.where(pos_scr[pl.ds(e, 1), :] == slot, 1.0, 0.0).astype(BF)
        res = _dot_nt(vals.astype(BF), onehot)
        idx_ref[pl.ds(e, 1), :] = (res[0:1] * float(IDX_SPLIT) + res[1:2]).astype(I32)
        gslot_ref[pl.ds(e, 1), :] = res[2:3] + res[3:4] + res[4:5]
        return carry

    lax.fori_loop(0, ne, expert_step, 0)


def _topk(aff_t, cap):
    nb, ne, s = aff_t.shape
    blk = pl.BlockSpec((None, ne, s), lambda b: (b, 0, 0))
    return pl.pallas_call(
        functools.partial(_topk_kernel, cap=cap),
        grid=(nb,),
        in_specs=[blk],
        out_specs=[blk,
                   pl.BlockSpec((None, ne, LANES), lambda b: (b, 0, 0)),
                   pl.BlockSpec((None, ne, cap), lambda b: (b, 0, 0)),
                   pl.BlockSpec((None, ne, cap), lambda b: (b, 0, 0))],
        out_shape=[jax.ShapeDtypeStruct((nb, ne, s), I32),
                   jax.ShapeDtypeStruct((nb, ne, LANES), I32),
                   jax.ShapeDtypeStruct((nb, ne, cap), I32),
                   jax.ShapeDtypeStruct((nb, ne, cap), F32)],
        scratch_shapes=[pltpu.VMEM((ne, s), I32)],
        compiler_params=_cparams("arbitrary"),
    )(aff_t)


GATHER_UNROLL = 8


def _token_copy(src_ref, src_tok, dst_ref, dst_tok, tok_rows, sem):
    src = pl.multiple_of(src_tok * tok_rows, tok_rows)
    dst = pl.multiple_of(dst_tok * tok_rows, tok_rows)
    return pltpu.make_async_copy(src_ref.at[pl.ds(src, tok_rows), :], dst_ref.at[pl.ds(dst, tok_rows), :], sem)


def _expert_up_kernel(cur_ref, nxt_ref, xp_ref, wg_ref, wu_ref, o_ref, raw_ref, xb_ref, sem, *, ne, n, nj):
    e = pl.program_id(0)
    j = pl.program_id(1)
    slot = e % 2
    per = n // nj
    tok_rows = raw_ref.shape[1] // n

    @pl.when((e == 0) & (j == 0))
    def _():
        def first(c, carry):
            _token_copy(xp_ref, cur_ref[0, c], raw_ref.at[0], c, tok_rows, sem.at[0]).start()
            return carry

        lax.fori_loop(0, n, first, 0, unroll=GATHER_UNROLL)

    def drain(which):
        def body(c, carry):
            _token_copy(xp_ref, 0, raw_ref.at[which], c, tok_rows, sem.at[which]).wait()
            return carry

        lax.fori_loop(0, n, body, 0, unroll=GATHER_UNROLL)

    @pl.when(j == 0)
    def _():
        drain(slot)
        d2 = tok_rows * LANES
        for q in range(tok_rows):
            w = raw_ref[slot, pl.ds(q, n, stride=tok_rows), :]
            xb_ref[:, q * LANES:(q + 1) * LANES] = lax.bitcast_convert_type(
                w & jnp.uint32(0xFFFF0000), F32).astype(BF)
            xb_ref[:, d2 + q * LANES:d2 + (q + 1) * LANES] = lax.bitcast_convert_type(w << 16, F32).astype(BF)

    for i in range(per):
        c = j * per + i
        _token_copy(xp_ref, nxt_ref[0, c], raw_ref.at[1 - slot], c, tok_rows, sem.at[1 - slot]).start()

    wg = wg_ref[...].astype(BF)
    wu = wu_ref[...].astype(BF)
    for r0 in range(0, n, n // 2):
        x = xb_ref[r0:r0 + n // 2]
        g = _dot(x, wg)
        u = _dot(x, wu)
        o_ref[r0:r0 + n // 2] = (g * jax.nn.sigmoid(g) * u).astype(BF)

    @pl.when((e == ne - 1) & (j == nj - 1))
    def _():
        drain(1 - slot)


def _expert_up(xp, rows, w_gate, w_up, layer):
    _, ne, d, f = w_gate.shape
    n = rows.shape[2]
    tf = min(f, 256)
    nj = f // tf
    tok_rows = d // 2 // LANES
    w_spec = pl.BlockSpec((None, None, d, tf), lambda e, j: (layer, e, 0, j))
    idx_spec = lambda f_: pl.BlockSpec((None, 1, n), f_, memory_space=pltpu.SMEM)
    return pl.pallas_call(
        functools.partial(_expert_up_kernel, ne=ne, n=n, nj=nj),
        grid=(ne, nj),
        in_specs=[idx_spec(lambda e, j: (e, 0, 0)),
                  idx_spec(lambda e, j: (jnp.minimum(e + 1, ne - 1), 0, 0)),
                  pl.BlockSpec(memory_space=pl.ANY), w_spec, w_spec],
        out_specs=pl.BlockSpec((None, n, tf), lambda e, j: (e, 0, j)),
        out_shape=jax.ShapeDtypeStruct((ne, n, f), BF),
        scratch_shapes=[pltpu.VMEM((2, n * tok_rows, LANES), U32), pltpu.VMEM((n, d), BF),
                        pltpu.SemaphoreType.DMA((2,))],
        compiler_params=_cparams("arbitrary", "arbitrary"),
    )(rows, rows, xp, w_gate, w_up)


def _expert_down_kernel(h_ref, w_ref, gs_ref, o_ref):
    o_ref[...] = _dot(h_ref[...], w_ref[...].astype(BF)) * gs_ref[...]


def _expert_down(hid, w_down, gslot, layer):
    ne, n, f = hid.shape
    d = w_down.shape[3]
    tn = min(d, 512)
    return pl.pallas_call(
        _expert_down_kernel,
        grid=(ne, d // tn),
        in_specs=[pl.BlockSpec((None, n, f), lambda e, j: (e, 0, 0)),
                  pl.BlockSpec((None, None, f, tn), lambda e, j: (layer, e, 0, j)),
                  pl.BlockSpec((None, n, 1), lambda e, j: (e, 0, 0))],
        out_specs=pl.BlockSpec((None, n, tn), lambda e, j: (e, 0, j)),
        out_shape=jax.ShapeDtypeStruct((ne, n, d), F32),
        compiler_params=_cparams("arbitrary", "arbitrary"),
    )(hid, w_down, gslot)


COMBINE_TOKENS = 128
COMBINE_CHUNK = 256
COMBINE_STATIC_CHUNKS = 2
SLAB_ROWS = (128, 64, 32, 16, 8)
COMBINE_SLOTS_PER_EXPERT = COMBINE_TOKENS + SUBLANES


def _slab_copy(y_ref, src_row, buf_ref, dst_row, rows, sem):
    return pltpu.make_async_copy(y_ref.at[pl.ds(src_row, rows), :], buf_ref.at[pl.ds(dst_row, rows), :], sem)


def _combine_kernel(cnt_ref, y_ref, pos_ref, res_ref, g2_ref, o_ref, buf_ref, acc_ref, slot_ref, sem,
                    *, ne, n, cap, nbound):
    b = pl.program_id(0)
    t = pl.program_id(1)
    nt = nbound - 1
    step = b * nt + t
    last = pl.num_programs(0) * nt - 1

    @pl.when(step == 0)
    def _():
        buf_ref[...] = jnp.zeros_like(buf_ref)

    def plan(bq, tq):
        slabs, shift = [], []
        total = jnp.int32(0)
        for e in range(ne):
            base = (bq * ne + e) * nbound + tq
            c0 = cnt_ref[base]
            c1 = cnt_ref[base + 1]
            a0 = (c0 // SUBLANES) * SUBLANES
            run = jnp.where(c1 > c0, (c1 + SUBLANES - 1) // SUBLANES * SUBLANES - a0, 0)
            row0 = e * n + bq * cap + a0
            done = jnp.int32(0)
            for rows in SLAB_ROWS:
                take = (run & rows) != 0
                slabs.append((take, rows, pl.multiple_of(row0 + done, SUBLANES),
                              pl.multiple_of(total + done, SUBLANES)))
                done = done + jnp.where(take, rows, 0)
            shift.append(total - a0)
            total = total + run
        return slabs, shift, total

    def start_all(slabs, half):
        for take, rows, src, dst in slabs:
            @pl.when(take)
            def _(rows=rows, src=src, dst=dst):
                _slab_copy(y_ref, src, buf_ref.at[half], dst, rows, sem.at[half]).start()

    half = step % 2
    slabs, shift, total = plan(b, t)

    @pl.when(step == 0)
    def _():
        start_all(slabs, half)

    nxt = jnp.minimum(step + 1, last)
    nxt_slabs, _, _ = plan(nxt // nt, nxt % nt)

    @pl.when(step < last)
    def _():
        start_all(nxt_slabs, 1 - half)

    for take, rows, _, _ in slabs:
        @pl.when(take)
        def _(rows=rows):
            _slab_copy(y_ref, 0, buf_ref.at[half], 0, rows, sem.at[half]).wait()

    pos = pos_ref[...]
    for e in range(ne):
        pe = pos[:, e:e + 1]
        where_e = jnp.where(pe >= 0, pe + shift[e], -1)
        slot_ref[e] = jnp.broadcast_to(where_e, (COMBINE_TOKENS, LANES))

    def expand(start):
        parts = []
        for q in range(COMBINE_CHUNK // LANES):
            lane = lax.broadcasted_iota(I32, (COMBINE_TOKENS, LANES), 1) + (start + q * LANES)
            hot = jnp.zeros((COMBINE_TOKENS, LANES), F32)
            for e in range(ne):
                hot = jnp.where(slot_ref[e] == lane, 1.0, hot)
            parts.append(hot.astype(BF))
        onehot = jnp.concatenate(parts, axis=1)
        live = lax.broadcasted_iota(I32, (COMBINE_CHUNK, 1), 0) + start < total
        rows = jnp.where(live, buf_ref[half, pl.ds(start, COMBINE_CHUNK), :], 0.0)
        r_hi, r_lo = _split_bf16(rows)
        return _dot(onehot, r_hi) + _dot(onehot, r_lo)

    acc_ref[...] = sum(expand(k * COMBINE_CHUNK) for k in range(COMBINE_STATIC_CHUNKS))

    def chunk(ci, carry):
        acc_ref[...] += expand(pl.multiple_of(ci * COMBINE_CHUNK, COMBINE_CHUNK))
        return carry

    lax.fori_loop(COMBINE_STATIC_CHUNKS, (total + COMBINE_CHUNK - 1) // COMBINE_CHUNK, chunk, 0)
    o_ref[...] = res_ref[...] + g2_ref[...] * acc_ref[...]


def _combine(cnt, y, pos_t, res, g2, nb, cap):
    m, d = res.shape
    s = m // nb
    ne = pos_t.shape[1]
    n = nb * cap
    nt = s // COMBINE_TOKENS
    nbound = nt + 1
    tok = lambda b, t, c: (b * nt + t, 0)
    grid_spec = pltpu.PrefetchScalarGridSpec(
        num_scalar_prefetch=1,
        grid=(nb, nt),
        in_specs=[pl.BlockSpec(memory_space=pl.ANY),
                  pl.BlockSpec((COMBINE_TOKENS, ne), tok),
                  pl.BlockSpec((COMBINE_TOKENS, d), tok),
                  pl.BlockSpec((None, 1, d), lambda b, t, c: (b, 0, 0))],
        out_specs=pl.BlockSpec((COMBINE_TOKENS, d), tok),
        scratch_shapes=[pltpu.VMEM((2, pl.cdiv(ne * COMBINE_SLOTS_PER_EXPERT, COMBINE_CHUNK) * COMBINE_CHUNK, d), F32),
                        pltpu.VMEM((COMBINE_TOKENS, d), F32),
                        pltpu.VMEM((ne, COMBINE_TOKENS, LANES), I32),
                        pltpu.SemaphoreType.DMA((2,))],
    )
    return pl.pallas_call(
        functools.partial(_combine_kernel, ne=ne, n=n, cap=cap, nbound=nbound),
        grid_spec=grid_spec,
        out_shape=jax.ShapeDtypeStruct((m, d), F32),
        compiler_params=_cparams("arbitrary", "arbitrary"),
    )(cnt[:, :, :nbound].reshape(-1), y, pos_t, res, g2)


def _moe(h2, nb, g, sh, sc, gate2, w_router, w_gate, w_up, w_down, layer):
    m, d = h2.shape
    s = m // nb
    ne = w_router.shape[1]
    cap = EC_FACTOR * s // ne
    xp, aff_t = _router(h2, nb, g, sh, sc, w_router.T)
    pos, cnt, idx, gslot = _topk(aff_t, cap)
    rows = idx + (jnp.arange(nb, dtype=I32) * s)[:, None, None]
    rows = jnp.swapaxes(rows, 0, 1).reshape(ne, 1, nb * cap)
    gslot = jnp.swapaxes(gslot, 0, 1).reshape(ne, nb * cap, 1)
    hid = _expert_up(xp, rows, w_gate, w_up, layer)
    y = _expert_down(hid, w_down, gslot, layer).reshape(ne * nb * cap, d)
    pos_t = jnp.swapaxes(pos, 1, 2).reshape(m, ne)
    return _combine(cnt, y, pos_t, h2, gate2, nb, cap)


def _rope_tables(s):
    rows = s // GRID_W
    r, col = jnp.meshgrid(jnp.arange(rows, dtype=F32), jnp.arange(GRID_W, dtype=F32), indexing="ij")
    r, col = r.reshape(-1), col.reshape(-1)
    n_pairs = HEAD_DIM // 4
    inv = ROPE_BASE ** (-jnp.arange(n_pairs, dtype=F32) / n_pairs)
    ang_r = r[:, None] * inv[None, :]
    ang_c = col[:, None] * inv[None, :]
    ang = jnp.concatenate([ang_r, ang_r, ang_c, ang_c], axis=-1)
    cos, sin = jnp.cos(ang), jnp.sin(ang)
    first = (jnp.arange(HEAD_DIM) % (HEAD_DIM // 2)) < HEAD_DIM // 4
    sin_a = jnp.where(first, -sin, 0.0)
    sin_b = jnp.where(first, 0.0, sin)
    two = lambda a: jnp.concatenate([a, a], axis=-1)
    return two(cos), two(sin_a), two(sin_b)


def kernel(x, c, ctx, c_ctx, ada_w, ada_b, norm1_g, norm2_g, attn_w_q, attn_w_kv, attn_w_o, attn_q_gain, attn_k_gain, attn_sink, conv_w_pw1, conv_b_pw1, conv_w_dw, conv_b_dw, conv_ln_g, conv_ln_b, conv_w_pw2, conv_b_pw2, moe_router, moe_w_gate, moe_w_up, moe_w_down):
    nb, s, d = x.shape
    lc = ctx.shape[1]
    depth = ada_w.shape[0]
    assert depth == 2, "layer 0 attention, layer 1 convolution; the context stream is never updated"
    m = nb * s

    cin = jnp.concatenate([c, c_ctx[None, :], jnp.zeros((8 - nb - 1, d), F32)], axis=0)
    mod = _ada(cin, ada_w, ada_b)

    def mods(layer, row0, nrow):
        return [mod[layer, row0:row0 + nrow, k * d:(k + 1) * d].reshape(nrow, 1, d) for k in range(6)]

    h = x.reshape(m, d)
    two = lambda a: jnp.concatenate([a, a], axis=-1).reshape(1, LANES)

    sh1, sc1, g1, sh2, sc2, g2 = mods(0, 0, nb)
    sh1c, sc1c = mods(0, nb, 1)[:2]
    n1 = norm1_g[0].reshape(1, d)
    wq = attn_w_q[0].astype(BF)
    wkv = attn_w_kv[0].astype(BF)
    wo = attn_w_o[0].astype(BF)
    q_gain = two(attn_q_gain[0]) * (HEAD_DIM ** -0.5 * LOG2E)
    k_gain = two(attn_k_gain[0])
    q, k, v = _qkv_proj(h, nb, n1, sh1, sc1, wq, wkv, q_gain, k_gain, _rope_tables(s))
    kx, vx = _qkv_proj(ctx.reshape(nb * lc, d), nb, n1, sh1c, sc1c, None, wkv, None, k_gain, None)
    att = _attention(q, k, v, kx, vx, attn_sink[0])
    h = _mm_res(att, wo, h, g1, nb)
    h = _moe(h, nb, norm2_g[0].reshape(1, d), sh2, sc2, g2,
             moe_router[0], moe_w_gate, moe_w_up, moe_w_down, 0)

    sh1, sc1, g1, sh2, sc2, g2 = mods(1, 0, nb)
    u = _pw1_glu(h, nb, norm1_g[1].reshape(1, d), sh1, sc1,
                 conv_w_pw1[0].astype(BF), conv_b_pw1[0].reshape(1, 2 * d))
    h = _conv_pw2(u, nb, conv_w_dw[0], conv_b_dw[0].reshape(1, d), conv_ln_g[0].reshape(1, d),
                  conv_ln_b[0].reshape(1, d), conv_w_pw2[0].astype(BF), conv_b_pw2[0].reshape(1, d), h, g1)
    h = _moe(h, nb, norm2_g[1].reshape(1, d), sh2, sc2, g2,
             moe_router[1], moe_w_gate, moe_w_up, moe_w_down, 1)
    return h.reshape(nb, s, d)
```

```python
import functools

import jax
import jax.numpy as jnp
from jax import lax
from jax.experimental import pallas as pl
from jax.experimental.pallas import tpu as pltpu

F32 = jnp.float32
BF = jnp.bfloat16
I32 = jnp.int32
U32 = jnp.uint32

HEAD_DIM = 64
GROUP = 4
GRID_W = 64
ROPE_BASE = 10000.0
ATT_BLOCK = 128
CONV_WIDTH = 31
CONV_HALO = 16
CONV_ROWS = 64
SUBLANES = 8
N_EXPERTS = 16
EC_FACTOR = 2
EPS = 1e-6
NEG_INF = -1e30
LOG2E = 1.4426950408889634
LANES = 128
V7X_VMEM_LIMIT = 56 * 1024 * 1024


def _cparams(*sem):
    return pltpu.CompilerParams(dimension_semantics=sem, vmem_limit_bytes=V7X_VMEM_LIMIT)


def _dot(a, b):
    return jnp.dot(a, b, preferred_element_type=F32)


def _dot_nt(a, b):
    return lax.dot_general(a, b, (((1,), (1,)), ((), ())), preferred_element_type=F32)


def _split_bf16(x):
    hi = x.astype(BF)
    lo = (x - hi.astype(F32)).astype(BF)
    return hi, lo


def _norm_mod(x, g, sh, sc):
    ms = jnp.mean(x * x, axis=-1, keepdims=True)
    return (x * lax.rsqrt(ms + EPS) * g) * (1.0 + sc) + sh


def _ada_kernel(c_ref, w_ref, b_ref, o_ref):
    c = c_ref[...]
    s = c * jax.nn.sigmoid(c)
    o_ref[...] = _dot(s.astype(BF), w_ref[...].astype(BF)) + b_ref[...]


def _ada(cin, ada_w, ada_b):
    depth, d, n = ada_w.shape
    rows = cin.shape[0]
    tn = min(n, 1024)
    return pl.pallas_call(
        _ada_kernel,
        grid=(depth, n // tn),
        in_specs=[pl.BlockSpec((rows, d), lambda l, j: (0, 0)),
                  pl.BlockSpec((None, d, tn), lambda l, j: (l, 0, j)),
                  pl.BlockSpec((None, 1, tn), lambda l, j: (l, 0, j))],
        out_specs=pl.BlockSpec((None, rows, tn), lambda l, j: (l, 0, j)),
        out_shape=jax.ShapeDtypeStruct((depth, rows, n), F32),
        compiler_params=_cparams("arbitrary", "arbitrary"),
    )(cin, ada_w, ada_b.reshape(depth, 1, n))


def _head_norm(y, gain):
    w = y.shape[1]
    r = lax.broadcasted_iota(I32, (w, w), 0) // HEAD_DIM
    c = lax.broadcasted_iota(I32, (w, w), 1) // HEAD_DIM
    bd = jnp.where(r == c, 1.0 / HEAD_DIM, 0.0).astype(BF)
    hi, lo = _split_bf16(y * y)
    ms = _dot(hi, bd) + _dot(lo, bd)
    return y * lax.rsqrt(ms + EPS) * jnp.concatenate([gain] * (w // LANES), axis=1)


def _rope(y, cos, sin_a, sin_b):
    return (y * cos + pltpu.roll(y, LANES - HEAD_DIM // 4, 1) * sin_a
            + pltpu.roll(y, HEAD_DIM // 4, 1) * sin_b)


def _qkv_kernel(*refs, nh, nkv, rope, with_q):
    if with_q:
        (x_ref, g_ref, sh_ref, sc_ref, wq_ref, wkv_ref, qg_ref, kg_ref,
         cos_ref, sa_ref, sb_ref, q_ref, k_ref, v_ref) = refs
    else:
        (x_ref, g_ref, sh_ref, sc_ref, wkv_ref, kg_ref, k_ref, v_ref) = refs
    hb = _norm_mod(x_ref[...], g_ref[...], sh_ref[...], sc_ref[...]).astype(BF)
    if rope:
        cos, sin_a, sin_b = cos_ref[...], sa_ref[...], sb_ref[...]
    kvd = nkv * HEAD_DIM
    kv = _dot(hb, wkv_ref[...])
    def finish(y_wide, gain, out_ref, head0):
        yn = _head_norm(y_wide, gain)
        for part in range(y_wide.shape[1] // LANES):
            y = yn[:, part * LANES:(part + 1) * LANES]
            if rope:
                y = _rope(y, cos, sin_a, sin_b)
            yb = y.astype(BF)
            out_ref[head0 + 2 * part] = yb[:, :HEAD_DIM]
            out_ref[head0 + 2 * part + 1] = yb[:, HEAD_DIM:]

    cw = min(2 * LANES, kvd)
    per = cw // HEAD_DIM
    for c in range(kvd // cw):
        finish(kv[:, c * cw:(c + 1) * cw], kg_ref[...], k_ref, per * c)
    for h in range(nkv):
        v_ref[h] = kv[:, kvd + h * HEAD_DIM:kvd + (h + 1) * HEAD_DIM].astype(BF)
    if with_q:
        nq = nh * HEAD_DIM // cw
        ahead = _dot(hb, wq_ref[:, :cw])
        for c in range(nq):
            qc = ahead
            if c + 1 < nq:
                ahead = _dot(hb, wq_ref[:, (c + 1) * cw:(c + 2) * cw])
            finish(qc, qg_ref[...], q_ref, per * c)


def _qkv_proj(x2, nb, g, sh, sc, wq, wkv, qg, kg, tables):
    m, d = x2.shape
    s = m // nb
    nkv = wkv.shape[1] // (2 * HEAD_DIM)
    with_q = wq is not None
    tm = min(s, 512)
    nt = s // tm
    shared = sh.shape[0] == 1
    mod_map = (lambda b, i: (0, 0, 0)) if shared else (lambda b, i: (b, 0, 0))
    row_spec = pl.BlockSpec((tm, d), lambda b, i: (b * nt + i, 0))
    vec_d = pl.BlockSpec((1, d), lambda b, i: (0, 0))
    mod_spec = pl.BlockSpec((None, 1, d), mod_map)
    vec_l = pl.BlockSpec((1, LANES), lambda b, i: (0, 0))
    tab_spec = pl.BlockSpec((tm, LANES), lambda b, i: (i, 0))
    kv_spec = pl.BlockSpec((None, nkv, tm, HEAD_DIM), lambda b, i: (b, 0, i, 0))
    kv_shape = jax.ShapeDtypeStruct((nb, nkv, s, HEAD_DIM), BF)
    if with_q:
        nh = wq.shape[1] // HEAD_DIM
        ins = [x2, g, sh, sc, wq, wkv, qg, kg, *tables]
        in_specs = [row_spec, vec_d, mod_spec, mod_spec,
                    pl.BlockSpec(wq.shape, lambda b, i: (0, 0)),
                    pl.BlockSpec(wkv.shape, lambda b, i: (0, 0)),
                    vec_l, vec_l, tab_spec, tab_spec, tab_spec]
        out_specs = [pl.BlockSpec((None, nh, tm, HEAD_DIM), lambda b, i: (b, 0, i, 0)), kv_spec, kv_spec]
        out_shape = [jax.ShapeDtypeStruct((nb, nh, s, HEAD_DIM), BF), kv_shape, kv_shape]
    else:
        nh = 0
        ins = [x2, g, sh, sc, wkv, kg]
        in_specs = [row_spec, vec_d, mod_spec, mod_spec,
                    pl.BlockSpec(wkv.shape, lambda b, i: (0, 0)), vec_l]
        out_specs = [kv_spec, kv_spec]
        out_shape = [kv_shape, kv_shape]
    return pl.pallas_call(
        functools.partial(_qkv_kernel, nh=nh, nkv=nkv, rope=with_q, with_q=with_q),
        grid=(nb, nt), in_specs=in_specs, out_specs=out_specs, out_shape=out_shape,
        compiler_params=_cparams("arbitrary", "arbitrary"),
    )(*ins)


def _attn_kernel(sink_ref, q_ref, kp_ref, kc_ref, kn_ref, vp_ref, vc_ref, vn_ref,
                 kx_ref, vx_ref, o_ref, bp_ref, bn_ref, kcat_ref, vcat_ref, *, nkv, nblk):
    i = pl.program_id(1)
    rows = GROUP * ATT_BLOCK
    w = ATT_BLOCK
    for h in range(nkv):
        for part, (k_ref, v_ref) in enumerate(((kp_ref, vp_ref), (kc_ref, vc_ref), (kn_ref, vn_ref))):
            kcat_ref[h, part * w:(part + 1) * w] = k_ref[h]
            vcat_ref[h, part * w:(part + 1) * w] = v_ref[h]
        kcat_ref[h, 3 * w:] = kx_ref[h]
        vcat_ref[h, 3 * w:] = vx_ref[h]
    qi = lax.broadcasted_iota(I32, (rows, ATT_BLOCK), 0) % ATT_BLOCK
    kj = lax.broadcasted_iota(I32, (rows, ATT_BLOCK), 1)
    prev_ok = (kj >= qi) & (i > 0)
    next_ok = (kj <= qi) & (i < nblk - 1)
    bp_ref[...] = jnp.where(prev_ok, 0.0, NEG_INF)
    bn_ref[...] = jnp.where(next_ok, 0.0, NEG_INF)
    def scores(h):
        q = q_ref[GROUP * h:GROUP * (h + 1)].reshape(rows, HEAD_DIM)
        return _dot_nt(q, kcat_ref[h])

    ahead = scores(0)
    for h in range(nkv):
        raw = ahead
        if h + 1 < nkv:
            ahead = scores(h + 1)
        s = jnp.concatenate([raw[:, :w] + bp_ref[...], raw[:, w:2 * w], raw[:, 2 * w:3 * w] + bn_ref[...],
                             raw[:, 3 * w:]], axis=1)
        sink = jnp.concatenate(
            [jnp.full((ATT_BLOCK, 1), sink_ref[GROUP * h + g] * LOG2E, F32) for g in range(GROUP)], axis=0)
        m = jnp.maximum(jnp.max(s, axis=-1, keepdims=True), sink)
        p = jnp.exp2(s - m)
        den = jnp.sum(p, axis=-1, keepdims=True) + jnp.exp2(sink - m)
        o = (_dot(p.astype(BF), vcat_ref[h]) / den).astype(BF)
        for g in range(GROUP):
            hh = GROUP * h + g
            o_ref[:, hh * HEAD_DIM:(hh + 1) * HEAD_DIM] = o[g * ATT_BLOCK:(g + 1) * ATT_BLOCK]


def _attention(q, k, v, kx, vx, sink):
    nb, nh, s, _ = q.shape
    nkv = k.shape[1]
    lc = kx.shape[2]
    nblk = s // ATT_BLOCK
    blk = lambda f: pl.BlockSpec((None, nkv, ATT_BLOCK, HEAD_DIM), f)
    prev = lambda b, i: (b, 0, jnp.maximum(i - 1, 0), 0)
    cur = lambda b, i: (b, 0, i, 0)
    nxt = lambda b, i: (b, 0, jnp.minimum(i + 1, nblk - 1), 0)
    ctx_spec = pl.BlockSpec((None, nkv, lc, HEAD_DIM), lambda b, i: (b, 0, 0, 0))
    return pl.pallas_call(
        functools.partial(_attn_kernel, nkv=nkv, nblk=nblk),
        grid=(nb, nblk),
        in_specs=[pl.BlockSpec(memory_space=pltpu.SMEM),
                  pl.BlockSpec((None, nh, ATT_BLOCK, HEAD_DIM), cur),
                  blk(prev), blk(cur), blk(nxt), blk(prev), blk(cur), blk(nxt), ctx_spec, ctx_spec],
        out_specs=pl.BlockSpec((ATT_BLOCK, nh * HEAD_DIM), lambda b, i: (b * nblk + i, 0)),
        out_shape=jax.ShapeDtypeStruct((nb * s, nh * HEAD_DIM), BF),
        scratch_shapes=[pltpu.VMEM((GROUP * ATT_BLOCK, ATT_BLOCK), F32)] * 2
        + [pltpu.VMEM((nkv, 3 * ATT_BLOCK + lc, HEAD_DIM), BF)] * 2,
        compiler_params=_cparams("arbitrary", "arbitrary"),
    )(sink, q, k, k, k, v, v, v, kx, vx)


def _mm_res_kernel(a_ref, w_ref, res_ref, gate_ref, o_ref):
    o_ref[...] = res_ref[...] + gate_ref[...] * _dot(a_ref[...], w_ref[...])


def _mm_res(a, w, res, gate, nb):
    m, kd = a.shape
    n = w.shape[1]
    tm = min(m // nb, 512)
    per_b = m // nb // tm
    return pl.pallas_call(
        _mm_res_kernel,
        grid=(m // tm,),
        in_specs=[pl.BlockSpec((tm, kd), lambda i: (i, 0)),
                  pl.BlockSpec((kd, n), lambda i: (0, 0)),
                  pl.BlockSpec((tm, n), lambda i: (i, 0)),
                  pl.BlockSpec((None, 1, n), lambda i: (i // per_b, 0, 0))],
        out_specs=pl.BlockSpec((tm, n), lambda i: (i, 0)),
        out_shape=jax.ShapeDtypeStruct((m, n), F32),
        compiler_params=_cparams("arbitrary"),
    )(a, w, res, gate)


def _pw1_kernel(x_ref, g_ref, sh_ref, sc_ref, wa_ref, wg_ref, ba_ref, bg_ref, o_ref):
    tm = x_ref.shape[0]
    for r0 in range(0, tm, tm // 2):
        rows = slice(r0, r0 + tm // 2)
        hb = _norm_mod(x_ref[rows], g_ref[...], sh_ref[...], sc_ref[...]).astype(BF)
        a = _dot(hb, wa_ref[...]) + ba_ref[...]
        gt = _dot(hb, wg_ref[...]) + bg_ref[...]
        o_ref[rows] = a * jax.nn.sigmoid(gt)


def _pw1_glu(x2, nb, g, sh, sc, w, bias):
    m, d = x2.shape
    tm = min(m // nb, 512)
    per_b = m // nb // tm
    tn = min(d, 1024)
    nj = d // tn
    mod_spec = pl.BlockSpec((None, 1, d), lambda j, i: (i // per_b, 0, 0))
    return pl.pallas_call(
        _pw1_kernel,
        grid=(nj, m // tm),
        in_specs=[pl.BlockSpec((tm, d), lambda j, i: (i, 0)),
                  pl.BlockSpec((1, d), lambda j, i: (0, 0)), mod_spec, mod_spec,
                  pl.BlockSpec((d, tn), lambda j, i: (0, j)),
                  pl.BlockSpec((d, tn), lambda j, i: (0, nj + j)),
                  pl.BlockSpec((1, tn), lambda j, i: (0, j)),
                  pl.BlockSpec((1, tn), lambda j, i: (0, nj + j))],
        out_specs=pl.BlockSpec((tm, tn), lambda j, i: (i, j)),
        out_shape=jax.ShapeDtypeStruct((m, d), F32),
        compiler_params=_cparams("arbitrary", "arbitrary"),
    )(x2, g, sh, sc, w, w, bias, bias)


def _conv_kernel(up_ref, uc_ref, un_ref, wdw_ref, bdw_ref, lg_ref, lb_ref, w2_ref, b2_ref,
                 res_ref, gate_ref, o_ref, ext_ref, cv_ref, *, nt):
    i = pl.program_id(1)
    tm, d = uc_ref.shape
    zero = jnp.zeros((CONV_HALO, d), F32)
    ext_ref[0:CONV_HALO, :] = jnp.where(i > 0, up_ref[...], zero)
    ext_ref[CONV_HALO:CONV_HALO + tm, :] = uc_ref[...]
    ext_ref[CONV_HALO + tm:, :] = jnp.where(i < nt - 1, un_ref[...], zero)
    off = CONV_HALO - CONV_WIDTH // 2
    rb = min(tm, CONV_ROWS)
    span = rb + 2 * CONV_HALO
    for c in range(d // LANES):
        cols = slice(c * LANES, (c + 1) * LANES)
        wcol = wdw_ref[:, cols]
        for r0 in range(0, tm, rb):
            xin = ext_ref[r0:r0 + span, cols]
            shifted = {0: xin}
            acc = jnp.broadcast_to(bdw_ref[:, cols], (rb, LANES))
            for k in range(CONV_WIDTH):
                r, a = (off + k) % SUBLANES, (off + k) // SUBLANES
                if r not in shifted:
                    shifted[r] = pltpu.roll(xin, span - r, 0)
                acc = acc + wcol[k:k + 1] * shifted[r][SUBLANES * a:SUBLANES * a + rb]
            cv_ref[r0:r0 + rb, cols] = acc
    u = cv_ref[...]
    mu = jnp.mean(u, axis=-1, keepdims=True)
    var = jnp.mean(jnp.square(u - mu), axis=-1, keepdims=True)
    y = (u - mu) * lax.rsqrt(var + EPS) * lg_ref[...] + lb_ref[...]
    y = y * jax.nn.sigmoid(y)
    out = _dot(y.astype(BF), w2_ref[...]) + b2_ref[...]
    o_ref[...] = res_ref[...] + gate_ref[...] * out


def _conv_pw2(u, nb, wdw, bdw, lg, lb, w2, b2, res, gate):
    m, d = u.shape
    s = m // nb
    tm = min(s, 256)
    nt = s // tm
    hb = tm // CONV_HALO
    nhb = s // CONV_HALO
    vec = pl.BlockSpec((1, d), lambda b, i: (0, 0))
    tile = pl.BlockSpec((tm, d), lambda b, i: (b * nt + i, 0))
    return pl.pallas_call(
        functools.partial(_conv_kernel, nt=nt),
        grid=(nb, nt),
        in_specs=[pl.BlockSpec((CONV_HALO, d), lambda b, i: (b * nhb + jnp.maximum(i * hb - 1, 0), 0)),
                  tile,
                  pl.BlockSpec((CONV_HALO, d), lambda b, i: (b * nhb + jnp.minimum((i + 1) * hb, nhb - 1), 0)),
                  pl.BlockSpec((CONV_WIDTH, d), lambda b, i: (0, 0)),
                  vec, vec, vec,
                  pl.BlockSpec((d, d), lambda b, i: (0, 0)),
                  vec, tile,
                  pl.BlockSpec((None, 1, d), lambda b, i: (b, 0, 0))],
        out_specs=tile,
        out_shape=jax.ShapeDtypeStruct((m, d), F32),
        scratch_shapes=[pltpu.VMEM((tm + 2 * CONV_HALO, d), F32), pltpu.VMEM((tm, d), F32)],
        compiler_params=_cparams("arbitrary", "arbitrary"),
    )(u, u, u, wdw, bdw, lg, lb, w2, b2, res, gate)


def _router_kernel(x_ref, g_ref, sh_ref, sc_ref, wr_ref, xp_ref, aff_ref):
    hn = _norm_mod(x_ref[...], g_ref[...], sh_ref[...], sc_ref[...])
    h_hi, h_lo = _split_bf16(hn)
    w_hi, w_lo = _split_bf16(wr_ref[...])
    logits = _dot_nt(w_hi, h_hi) + _dot_nt(w_hi, h_lo) + _dot_nt(w_lo, h_hi)
    mx = jnp.max(logits, axis=0, keepdims=True)
    ex = jnp.exp(logits - mx)
    aff_ref[...] = ex / jnp.sum(ex, axis=0, keepdims=True)
    bits = lax.bitcast_convert_type(h_hi.astype(F32), U32)
    tm = bits.shape[0]
    d2 = bits.shape[1] // 2
    word = bits[:, :d2] | (bits[:, d2:] >> 16)
    tok_rows = d2 // LANES
    for j in range(tok_rows):
        xp_ref[pl.ds(j, tm, stride=tok_rows), :] = word[:, j * LANES:(j + 1) * LANES]


def _router(x2, nb, g, sh, sc, wr_t):
    m, d = x2.shape
    s = m // nb
    ne = wr_t.shape[0]
    tm = min(s, 512)
    nt = s // tm
    tok_rows = d // 2 // LANES
    mod_spec = pl.BlockSpec((None, 1, d), lambda b, i: (b, 0, 0))
    return pl.pallas_call(
        _router_kernel,
        grid=(nb, nt),
        in_specs=[pl.BlockSpec((tm, d), lambda b, i: (b * nt + i, 0)),
                  pl.BlockSpec((1, d), lambda b, i: (0, 0)), mod_spec, mod_spec,
                  pl.BlockSpec((ne, d), lambda b, i: (0, 0))],
        out_specs=[pl.BlockSpec((tm * tok_rows, LANES), lambda b, i: (b * nt + i, 0)),
                   pl.BlockSpec((None, ne, tm), lambda b, i: (b, 0, i))],
        out_shape=[jax.ShapeDtypeStruct((m * tok_rows, LANES), U32),
                   jax.ShapeDtypeStruct((nb, ne, s), F32)],
        compiler_params=_cparams("arbitrary", "arbitrary"),
    )(x2, g, sh, sc, wr_t)


IDX_SPLIT = 64


def _prefix_incl(mask_bf):
    ne, s = mask_bf.shape
    r = lax.broadcasted_iota(I32, (LANES, LANES), 0)
    c = lax.broadcasted_iota(I32, (LANES, LANES), 1)
    tri = jnp.where(r <= c, 1.0, 0.0).astype(BF)
    carry = jnp.zeros((ne, 1), F32)
    out = []
    for ch in range(s // LANES):
        inc = _dot(mask_bf[:, ch * LANES:(ch + 1) * LANES], tri) + carry
        out.append(inc)
        carry = inc[:, LANES - 1:LANES]
    return jnp.concatenate(out, axis=1)


def _topk_kernel(aff_ref, pos_ref, cnt_ref, idx_ref, gslot_ref, pos_scr, *, cap):
    aff = aff_ref[...]
    ne, s = aff.shape
    v = lax.bitcast_convert_type(aff, I32)

    def bit_step(it, prefix):
        cand = prefix | jnp.left_shift(jnp.int32(1), 30 - it)
        cnt = jnp.sum(jnp.where(v >= cand, 1.0, 0.0), axis=1, keepdims=True)
        return jnp.where(cnt >= cap, cand, prefix)

    thr = lax.fori_loop(0, 31, bit_step, jnp.zeros((ne, 1), I32))
    gt = v > thr
    eq = v == thr
    need = cap - jnp.sum(jnp.where(gt, 1.0, 0.0), axis=1, keepdims=True)
    eq_bf = jnp.where(eq, 1.0, 0.0).astype(BF)
    eq_rank = _prefix_incl(eq_bf) - eq_bf.astype(F32)
    sel = gt | (eq & (eq_rank < need))
    sel_bf = jnp.where(sel, 1.0, 0.0).astype(BF)
    incl = _prefix_incl(sel_bf)
    pos = jnp.where(sel, incl - 1.0, -1.0).astype(I32)
    pos_ref[...] = pos
    t_i = lax.broadcasted_iota(I32, (s, LANES), 0)
    j_i = lax.broadcasted_iota(I32, (s, LANES), 1)
    before = jnp.where(t_i < j_i * LANES, 1.0, 0.0).astype(BF)
    cnt_ref[...] = _dot(sel_bf, before).astype(I32)
    pos_scr[...] = pos
    t_row = lax.broadcasted_iota(I32, (SUBLANES, s), 1)
    r_row = lax.broadcasted_iota(I32, (SUBLANES, s), 0)
    tval = jnp.where(r_row == 0, t_row // IDX_SPLIT, jnp.where(r_row == 1, t_row % IDX_SPLIT, 0)).astype(F32)
    slot = lax.broadcasted_iota(I32, (cap, 1), 0)

    def expert_step(e, carry):
        a = aff_ref[pl.ds(e, 1), :]
        a_hi = a.astype(BF).astype(F32)
        a_mid = (a - a_hi).astype(BF).astype(F32)
        a_lo = a - a_hi - a_mid
        vals = jnp.where(r_row == 2, a_hi, jnp.where(r_row == 3, a_mid, jnp.where(r_row == 4, a_lo, tval)))
        onehot = jnp.where(pos_scr[pl.ds(e, 1), :] == slot, 1.0, 0.0).astype(BF)
        res = _dot_nt(vals.astype(BF), onehot)
        idx_ref[pl.ds(e, 1), :] = (res[0:1] * float(IDX_SPLIT) + res[1:2]).astype(I32)
        gslot_ref[pl.ds(e, 1), :] = res[2:3] + res[3:4] + res[4:5]
        return carry

    lax.fori_loop(0, ne, expert_step, 0)


def _topk(aff_t, cap):
    nb, ne, s = aff_t.shape
    blk = pl.BlockSpec((None, ne, s), lambda b: (b, 0, 0))
    return pl.pallas_call(
        functools.partial(_topk_kernel, cap=cap),
        grid=(nb,),
        in_specs=[blk],
        out_specs=[blk,
                   pl.BlockSpec((None, ne, LANES), lambda b: (b, 0, 0)),
                   pl.BlockSpec((None, ne, cap), lambda b: (b, 0, 0)),
                   pl.BlockSpec((None, ne, cap), lambda b: (b, 0, 0))],
        out_shape=[jax.ShapeDtypeStruct((nb, ne, s), I32),
                   jax.ShapeDtypeStruct((nb, ne, LANES), I32),
                   jax.ShapeDtypeStruct((nb, ne, cap), I32),
                   jax.ShapeDtypeStruct((nb, ne, cap), F32)],
        scratch_shapes=[pltpu.VMEM((ne, s), I32)],
        compiler_params=_cparams("arbitrary"),
    )(aff_t)


GATHER_UNROLL = 8


def _token_copy(src_ref, src_tok, dst_ref, dst_tok, tok_rows, sem):
    src = pl.multiple_of(src_tok * tok_rows, tok_rows)
    dst = pl.multiple_of(dst_tok * tok_rows, tok_rows)
    return pltpu.make_async_copy(src_ref.at[pl.ds(src, tok_rows), :], dst_ref.at[pl.ds(dst, tok_rows), :], sem)


def _expert_up_kernel(cur_ref, nxt_ref, xp_ref, wg_ref, wu_ref, o_ref, raw_ref, xb_ref, sem, *, ne, n, nj):
    e = pl.program_id(0)
    j = pl.program_id(1)
    slot = e % 2
    per = n // nj
    tok_rows = raw_ref.shape[1] // n

    @pl.when((e == 0) & (j == 0))
    def _():
        def first(c, carry):
            _token_copy(xp_ref, cur_ref[0, c], raw_ref.at[0], c, tok_rows, sem.at[0]).start()
            return carry

        lax.fori_loop(0, n, first, 0, unroll=GATHER_UNROLL)

    def drain(which):
        def body(c, carry):
            _token_copy(xp_ref, 0, raw_ref.at[which], c, tok_rows, sem.at[which]).wait()
            return carry

        lax.fori_loop(0, n, body, 0, unroll=GATHER_UNROLL)

    @pl.when(j == 0)
    def _():
        drain(slot)
        d2 = tok_rows * LANES
        for q in range(tok_rows):
            w = raw_ref[slot, pl.ds(q, n, stride=tok_rows), :]
            xb_ref[:, q * LANES:(q + 1) * LANES] = lax.bitcast_convert_type(
                w & jnp.uint32(0xFFFF0000), F32).astype(BF)
            xb_ref[:, d2 + q * LANES:d2 + (q + 1) * LANES] = lax.bitcast_convert_type(w << 16, F32).astype(BF)

    for i in range(per):
        c = j * per + i
        _token_copy(xp_ref, nxt_ref[0, c], raw_ref.at[1 - slot], c, tok_rows, sem.at[1 - slot]).start()

    wg = wg_ref[...].astype(BF)
    wu = wu_ref[...].astype(BF)
    for r0 in range(0, n, n // 2):
        x = xb_ref[r0:r0 + n // 2]
        g = _dot(x, wg)
        u = _dot(x, wu)
        o_ref[r0:r0 + n // 2] = (g * jax.nn.sigmoid(g) * u).astype(BF)

    @pl.when((e == ne - 1) & (j == nj - 1))
    def _():
        drain(1 - slot)


def _expert_up(xp, rows, w_gate, w_up, layer):
    _, ne, d, f = w_gate.shape
    n = rows.shape[2]
    tf = min(f, 512)
    nj = f // tf
    tok_rows = d // 2 // LANES
    w_spec = pl.BlockSpec((None, None, d, tf), lambda e, j: (layer, e, 0, j))
    idx_spec = lambda f_: pl.BlockSpec((None, 1, n), f_, memory_space=pltpu.SMEM)
    return pl.pallas_call(
        functools.partial(_expert_up_kernel, ne=ne, n=n, nj=nj),
        grid=(ne, nj),
        in_specs=[idx_spec(lambda e, j: (e, 0, 0)),
                  idx_spec(lambda e, j: (jnp.minimum(e + 1, ne - 1), 0, 0)),
                  pl.BlockSpec(memory_space=pl.ANY), w_spec, w_spec],
        out_specs=pl.BlockSpec((None, n, tf), lambda e, j: (e, 0, j)),
        out_shape=jax.ShapeDtypeStruct((ne, n, f), BF),
        scratch_shapes=[pltpu.VMEM((2, n * tok_rows, LANES), U32), pltpu.VMEM((n, d), BF),
                        pltpu.SemaphoreType.DMA((2,))],
        compiler_params=_cparams("arbitrary", "arbitrary"),
    )(rows, rows, xp, w_gate, w_up)


def _expert_down_kernel(h_ref, w_ref, gs_ref, o_ref):
    o_ref[...] = _dot(h_ref[...], w_ref[...].astype(BF)) * gs_ref[...]


def _expert_down(hid, w_down, gslot, layer):
    ne, n, f = hid.shape
    d = w_down.shape[3]
    tn = min(d, 512)
    return pl.pallas_call(
        _expert_down_kernel,
        grid=(ne, d // tn),
        in_specs=[pl.BlockSpec((None, n, f), lambda e, j: (e, 0, 0)),
                  pl.BlockSpec((None, None, f, tn), lambda e, j: (layer, e, 0, j)),
                  pl.BlockSpec((None, n, 1), lambda e, j: (e, 0, 0))],
        out_specs=pl.BlockSpec((None, n, tn), lambda e, j: (e, 0, j)),
        out_shape=jax.ShapeDtypeStruct((ne, n, d), F32),
        compiler_params=_cparams("arbitrary", "arbitrary"),
    )(hid, w_down, gslot)


COMBINE_TOKENS = 128
COMBINE_CHUNK = 256
COMBINE_STATIC_CHUNKS = 2
SLAB_ROWS = (128, 64, 32, 16, 8)
COMBINE_SLOTS_PER_EXPERT = COMBINE_TOKENS + SUBLANES


def _slab_copy(y_ref, src_row, buf_ref, dst_row, rows, sem):
    return pltpu.make_async_copy(y_ref.at[pl.ds(src_row, rows), :], buf_ref.at[pl.ds(dst_row, rows), :], sem)


def _combine_kernel(cnt_ref, y_ref, pos_ref, res_ref, g2_ref, o_ref, buf_ref, acc_ref, slot_ref, sem,
                    *, ne, n, cap, nbound):
    b = pl.program_id(0)
    t = pl.program_id(1)
    nt = nbound - 1
    step = b * nt + t
    last = pl.num_programs(0) * nt - 1

    @pl.when(step == 0)
    def _():
        buf_ref[...] = jnp.zeros_like(buf_ref)

    def plan(bq, tq):
        slabs, shift = [], []
        total = jnp.int32(0)
        for e in range(ne):
            base = (bq * ne + e) * nbound + tq
            c0 = cnt_ref[base]
            c1 = cnt_ref[base + 1]
            a0 = (c0 // SUBLANES) * SUBLANES
            run = jnp.where(c1 > c0, (c1 + SUBLANES - 1) // SUBLANES * SUBLANES - a0, 0)
            row0 = e * n + bq * cap + a0
            done = jnp.int32(0)
            for rows in SLAB_ROWS:
                take = (run & rows) != 0
                slabs.append((take, rows, pl.multiple_of(row0 + done, SUBLANES),
                              pl.multiple_of(total + done, SUBLANES)))
                done = done + jnp.where(take, rows, 0)
            shift.append(total - a0)
            total = total + run
        return slabs, shift, total

    def start_all(slabs, half):
        for take, rows, src, dst in slabs:
            @pl.when(take)
            def _(rows=rows, src=src, dst=dst):
                _slab_copy(y_ref, src, buf_ref.at[half], dst, rows, sem.at[half]).start()

    half = step % 2
    slabs, shift, total = plan(b, t)

    @pl.when(step == 0)
    def _():
        start_all(slabs, half)

    nxt = jnp.minimum(step + 1, last)
    nxt_slabs, _, _ = plan(nxt // nt, nxt % nt)

    @pl.when(step < last)
    def _():
        start_all(nxt_slabs, 1 - half)

    for take, rows, _, _ in slabs:
        @pl.when(take)
        def _(rows=rows):
            _slab_copy(y_ref, 0, buf_ref.at[half], 0, rows, sem.at[half]).wait()

    pos = pos_ref[...]
    for e in range(ne):
        pe = pos[:, e:e + 1]
        where_e = jnp.where(pe >= 0, pe + shift[e], -1)
        slot_ref[e] = jnp.broadcast_to(where_e, (COMBINE_TOKENS, LANES))

    def expand(start):
        parts = []
        for q in range(COMBINE_CHUNK // LANES):
            lane = lax.broadcasted_iota(I32, (COMBINE_TOKENS, LANES), 1) + (start + q * LANES)
            hot = jnp.zeros((COMBINE_TOKENS, LANES), F32)
            for e in range(ne):
                hot = jnp.where(slot_ref[e] == lane, 1.0, hot)
            parts.append(hot.astype(BF))
        onehot = jnp.concatenate(parts, axis=1)
        live = lax.broadcasted_iota(I32, (COMBINE_CHUNK, 1), 0) + start < total
        rows = jnp.where(live, buf_ref[half, pl.ds(start, COMBINE_CHUNK), :], 0.0)
        r_hi, r_lo = _split_bf16(rows)
        return _dot(onehot, r_hi) + _dot(onehot, r_lo)

    acc_ref[...] = sum(expand(k * COMBINE_CHUNK) for k in range(COMBINE_STATIC_CHUNKS))

    def chunk(ci, carry):
        acc_ref[...] += expand(pl.multiple_of(ci * COMBINE_CHUNK, COMBINE_CHUNK))
        return carry

    lax.fori_loop(COMBINE_STATIC_CHUNKS, (total + COMBINE_CHUNK - 1) // COMBINE_CHUNK, chunk, 0)
    o_ref[...] = res_ref[...] + g2_ref[...] * acc_ref[...]


def _combine(cnt, y, pos_t, res, g2, nb, cap):
    m, d = res.shape
    s = m // nb
    ne = pos_t.shape[1]
    n = nb * cap
    nt = s // COMBINE_TOKENS
    nbound = nt + 1
    tok = lambda b, t, c: (b * nt + t, 0)
    grid_spec = pltpu.PrefetchScalarGridSpec(
        num_scalar_prefetch=1,
        grid=(nb, nt),
        in_specs=[pl.BlockSpec(memory_space=pl.ANY),
                  pl.BlockSpec((COMBINE_TOKENS, ne), tok),
                  pl.BlockSpec((COMBINE_TOKENS, d), tok),
                  pl.BlockSpec((None, 1, d), lambda b, t, c: (b, 0, 0))],
        out_specs=pl.BlockSpec((COMBINE_TOKENS, d), tok),
        scratch_shapes=[pltpu.VMEM((2, pl.cdiv(ne * COMBINE_SLOTS_PER_EXPERT, COMBINE_CHUNK) * COMBINE_CHUNK, d), F32),
                        pltpu.VMEM((COMBINE_TOKENS, d), F32),
                        pltpu.VMEM((ne, COMBINE_TOKENS, LANES), I32),
                        pltpu.SemaphoreType.DMA((2,))],
    )
    return pl.pallas_call(
        functools.partial(_combine_kernel, ne=ne, n=n, cap=cap, nbound=nbound),
        grid_spec=grid_spec,
        out_shape=jax.ShapeDtypeStruct((m, d), F32),
        compiler_params=_cparams("arbitrary", "arbitrary"),
    )(cnt[:, :, :nbound].reshape(-1), y, pos_t, res, g2)


def _moe(h2, nb, g, sh, sc, gate2, w_router, w_gate, w_up, w_down, layer):
    m, d = h2.shape
    s = m // nb
    ne = w_router.shape[1]
    cap = EC_FACTOR * s // ne
    xp, aff_t = _router(h2, nb, g, sh, sc, w_router.T)
    pos, cnt, idx, gslot = _topk(aff_t, cap)
    rows = idx + (jnp.arange(nb, dtype=I32) * s)[:, None, None]
    rows = jnp.swapaxes(rows, 0, 1).reshape(ne, 1, nb * cap)
    gslot = jnp.swapaxes(gslot, 0, 1).reshape(ne, nb * cap, 1)
    hid = _expert_up(xp, rows, w_gate, w_up, layer)
    y = _expert_down(hid, w_down, gslot, layer).reshape(ne * nb * cap, d)
    pos_t = jnp.swapaxes(pos, 1, 2).reshape(m, ne)
    return _combine(cnt, y, pos_t, h2, gate2, nb, cap)


def _rope_tables(s):
    rows = s // GRID_W
    r, col = jnp.meshgrid(jnp.arange(rows, dtype=F32), jnp.arange(GRID_W, dtype=F32), indexing="ij")
    r, col = r.reshape(-1), col.reshape(-1)
    n_pairs = HEAD_DIM // 4
    inv = ROPE_BASE ** (-jnp.arange(n_pairs, dtype=F32) / n_pairs)
    ang_r = r[:, None] * inv[None, :]
    ang_c = col[:, None] * inv[None, :]
    ang = jnp.concatenate([ang_r, ang_r, ang_c, ang_c], axis=-1)
    cos, sin = jnp.cos(ang), jnp.sin(ang)
    first = (jnp.arange(HEAD_DIM) % (HEAD_DIM // 2)) < HEAD_DIM // 4
    sin_a = jnp.where(first, -sin, 0.0)
    sin_b = jnp.where(first, 0.0, sin)
    two = lambda a: jnp.concatenate([a, a], axis=-1)
    return two(cos), two(sin_a), two(sin_b)


def kernel(x, c, ctx, c_ctx, ada_w, ada_b, norm1_g, norm2_g, attn_w_q, attn_w_kv, attn_w_o, attn_q_gain, attn_k_gain, attn_sink, conv_w_pw1, conv_b_pw1, conv_w_dw, conv_b_dw, conv_ln_g, conv_ln_b, conv_w_pw2, conv_b_pw2, moe_router, moe_w_gate, moe_w_up, moe_w_down):
    nb, s, d = x.shape
    lc = ctx.shape[1]
    depth = ada_w.shape[0]
    assert depth == 2, "layer 0 attention, layer 1 convolution; the context stream is never updated"
    m = nb * s

    cin = jnp.concatenate([c, c_ctx[None, :], jnp.zeros((8 - nb - 1, d), F32)], axis=0)
    mod = _ada(cin, ada_w, ada_b)

    def mods(layer, row0, nrow):
        return [mod[layer, row0:row0 + nrow, k * d:(k + 1) * d].reshape(nrow, 1, d) for k in range(6)]

    h = x.reshape(m, d)
    two = lambda a: jnp.concatenate([a, a], axis=-1).reshape(1, LANES)

    sh1, sc1, g1, sh2, sc2, g2 = mods(0, 0, nb)
    sh1c, sc1c = mods(0, nb, 1)[:2]
    n1 = norm1_g[0].reshape(1, d)
    wq = attn_w_q[0].astype(BF)
    wkv = attn_w_kv[0].astype(BF)
    wo = attn_w_o[0].astype(BF)
    q_gain = two(attn_q_gain[0]) * (HEAD_DIM ** -0.5 * LOG2E)
    k_gain = two(attn_k_gain[0])
    q, k, v = _qkv_proj(h, nb, n1, sh1, sc1, wq, wkv, q_gain, k_gain, _rope_tables(s))
    kx, vx = _qkv_proj(ctx.reshape(nb * lc, d), nb, n1, sh1c, sc1c, None, wkv, None, k_gain, None)
    att = _attention(q, k, v, kx, vx, attn_sink[0])
    h = _mm_res(att, wo, h, g1, nb)
    h = _moe(h, nb, norm2_g[0].reshape(1, d), sh2, sc2, g2,
             moe_router[0], moe_w_gate, moe_w_up, moe_w_down, 0)

    sh1, sc1, g1, sh2, sc2, g2 = mods(1, 0, nb)
    u = _pw1_glu(h, nb, norm1_g[1].reshape(1, d), sh1, sc1,
                 conv_w_pw1[0].astype(BF), conv_b_pw1[0].reshape(1, 2 * d))
    h = _conv_pw2(u, nb, conv_w_dw[0], conv_b_dw[0].reshape(1, d), conv_ln_g[0].reshape(1, d),
                  conv_ln_b[0].reshape(1, d), conv_w_pw2[0].astype(BF), conv_b_pw2[0].reshape(1, d), h, g1)
    h = _moe(h, nb, norm2_g[1].reshape(1, d), sh2, sc2, g2,
             moe_router[1], moe_w_gate, moe_w_up, moe_w_down, 1)
    return h.reshape(nb, s, d)
```
